```python
import jax, jax.numpy as jnp
from jax import lax
import numpy as np

D_MODEL = 1024
BATCH = 1
SEQ = 16384
DEPTH = 4

N_A = DEPTH // 2
N_B = DEPTH - N_A
M_HEADS = 4
M_DQK = 128
M_DV = D_MODEL // M_HEADS
M_CHUNK = 64
M_SPLITS = [M_HEADS * M_DQK, 2 * M_HEADS * M_DQK, 2 * M_HEADS * M_DQK + M_HEADS * M_DV,
            2 * M_HEADS * M_DQK + 2 * M_HEADS * M_DV, 2 * M_HEADS * M_DQK + 2 * M_HEADS * M_DV + M_HEADS]
M_IN = 2 * M_HEADS * M_DQK + 2 * M_HEADS * M_DV + 2 * M_HEADS
A_HEADS = 8
A_DH = D_MODEL // A_HEADS
ROT_DIM = A_DH // 4
ROPE_THETA = 500000.0
MOBA_BLOCK = 256
MOBA_TOPK = 3
Q_BLOCK = 128
D_FF = 2816
CONV_W = 3
EPS = 1e-6

kernel_name = 'yoco_mlstm_moba_convffn'


def rmsnorm(x, g):
    xf = x.astype(jnp.float32)
    y = xf * lax.rsqrt(jnp.mean(xf * xf, axis=-1, keepdims=True) + EPS)
    return (y * g.astype(jnp.float32)).astype(x.dtype)


def rope_tables(s_len):
    pos = jnp.arange(s_len, dtype=jnp.float32)
    inv = ROPE_THETA ** (-jnp.arange(0, ROT_DIM, 2, dtype=jnp.float32) / ROT_DIM)
    ang = pos[:, None] * inv[None, :]
    return jnp.cos(ang), jnp.sin(ang)


def apply_partial_rope(t, cos, sin):
    half = ROT_DIM // 2
    t1, t2, rest = t[..., :half], t[..., half:ROT_DIM], t[..., ROT_DIM:]
    c = cos[None, :, None, :].astype(t.dtype)
    s = sin[None, :, None, :].astype(t.dtype)
    return jnp.concatenate([t1 * c - t2 * s, t2 * c + t1 * s, rest], axis=-1)


def mlstm_chunkwise(q, k, v, i_pre, f_pre):
    b_, s_, h_, dk = q.shape
    nc = s_ // M_CHUNK

    def to_chunks(t):
        t = t.astype(jnp.float32).reshape((b_, nc, M_CHUNK, h_) + t.shape[3:])
        return jnp.moveaxis(jnp.moveaxis(t, 1, 0), 3, 2)

    qc, kc, vc = to_chunks(q), to_chunks(k), to_chunks(v)
    ic = to_chunks(i_pre)
    lfc = to_chunks(jax.nn.log_sigmoid(f_pre.astype(jnp.float32)))
    tri = jnp.tril(jnp.ones((M_CHUNK, M_CHUNK), dtype=bool))

    def step(carry, inp):
        c_st, n_st, m_st = carry
        qb, kb, vb, ib, lfb = inp
        bcum = jnp.cumsum(lfb, axis=-1)
        dmat = jnp.where(tri, bcum[..., :, None] - bcum[..., None, :] + ib[..., None, :], -jnp.inf)
        inter = bcum + m_st[..., None]
        m_t = jnp.maximum(inter, jnp.max(dmat, axis=-1))
        w_inter = jnp.exp(inter - m_t)
        s = jnp.einsum('bhtd,bhsd->bhts', qb, kb) * jnp.exp(dmat - m_t[..., None])
        num = w_inter[..., None] * jnp.einsum('bhtd,bhde->bhte', qb, c_st) + jnp.einsum('bhts,bhse->bhte', s, vb)
        den = w_inter * jnp.einsum('bhtd,bhd->bht', qb, n_st) + jnp.sum(s, axis=-1)
        h = num / jnp.maximum(jnp.abs(den), jnp.exp(-m_t))[..., None]
        b_last = bcum[..., -1]
        g = b_last[..., None] - bcum + ib
        m_new = jnp.maximum(b_last + m_st, jnp.max(g, axis=-1))
        decay = jnp.exp(b_last + m_st - m_new)
        wk = jnp.exp(g - m_new[..., None])[..., None] * kb
        c_new = decay[..., None, None] * c_st + jnp.einsum('bhsd,bhse->bhde', wk, vb)
        n_new = decay[..., None] * n_st + jnp.sum(wk, axis=-2)
        return (c_new, n_new, m_new), h

    dv = v.shape[-1]
    init = (jnp.zeros((b_, h_, dk, dv), jnp.float32), jnp.zeros((b_, h_, dk), jnp.float32),
            jnp.zeros((b_, h_), jnp.float32))
    _, hs = lax.scan(step, init, (qc, kc, vc, ic, lfc))
    return jnp.transpose(hs, (1, 0, 3, 2, 4)).reshape(b_, s_, h_, dv)


def mlstm_mixer(x, norm_g, w_in, b_gates, h_norm_g, w_out):
    b_, s_, _ = x.shape
    u = rmsnorm(x, norm_g) @ w_in
    q, k, v, o, gi, gf = jnp.split(u, M_SPLITS, axis=-1)
    q = q.reshape(b_, s_, M_HEADS, M_DQK) * (M_DQK ** -0.5)
    k = k.reshape(b_, s_, M_HEADS, M_DQK)
    v = v.reshape(b_, s_, M_HEADS, M_DV)
    i_pre = gi.astype(jnp.float32) + b_gates[0]
    f_pre = gf.astype(jnp.float32) + b_gates[1]
    h = mlstm_chunkwise(q, k, v, i_pre, f_pre)
    h = rmsnorm(h, h_norm_g.reshape(M_HEADS, M_DV)).reshape(b_, s_, M_HEADS * M_DV)
    h = h * jax.nn.sigmoid(o.astype(jnp.float32))
    return h.astype(x.dtype) @ w_out


def conv_ffn(x, norm_g, w_up, conv_w, conv_b, w_down):
    s_ = x.shape[1]
    u = rmsnorm(x, norm_g) @ w_up
    up = jnp.pad(u, ((0, 0), (CONV_W - 1, 0), (0, 0)))
    c = conv_b + up[:, 0:s_] * conv_w[0]
    for j in range(1, CONV_W):
        c = c + up[:, j:j + s_] * conv_w[j]
    val, gate = jnp.split(c, 2, axis=-1)
    return (jax.nn.silu(gate) * val) @ w_down


def shared_kv(x, norm_g, w_kv, k_norm_g, cos, sin):
    b_, s_, _ = x.shape
    kv = rmsnorm(x, norm_g) @ w_kv
    k, v = jnp.split(kv, 2, axis=-1)
    k = apply_partial_rope(rmsnorm(k.reshape(b_, s_, A_HEADS, A_DH), k_norm_g), cos, sin)
    v = v.reshape(b_, s_, A_HEADS, A_DH)
    nb = -(-s_ // MOBA_BLOCK)
    pad = nb * MOBA_BLOCK - s_
    k = jnp.pad(k, ((0, 0), (0, pad), (0, 0), (0, 0)))
    v = jnp.pad(v, ((0, 0), (0, pad), (0, 0), (0, 0)))
    kb = k.reshape(b_, nb, MOBA_BLOCK, A_HEADS, A_DH).transpose(0, 3, 1, 2, 4)
    vb = v.reshape(b_, nb, MOBA_BLOCK, A_HEADS, A_DH).transpose(0, 3, 1, 2, 4)
    kmean = jnp.mean(kb.astype(jnp.float32), axis=3).astype(kb.dtype)
    return kb, vb, kmean


def moba_mixer(x, norm_g, w_q, q_norm_g, w_o, kb, vb, kmean, cos, sin):
    b_, s_, _ = x.shape
    q = (rmsnorm(x, norm_g) @ w_q).reshape(b_, s_, A_HEADS, A_DH)
    q = apply_partial_rope(rmsnorm(q, q_norm_g), cos, sin) * (A_DH ** -0.5)
    q = q.transpose(0, 2, 1, 3)
    nb = kb.shape[2]
    n_topk = min(MOBA_TOPK, nb)
    bidx = jnp.arange(b_)[:, None, None, None]
    hidx = jnp.arange(A_HEADS)[None, :, None, None]
    n_sel = n_topk * MOBA_BLOCK

    def attend_block(qi):
        start = qi * Q_BLOCK
        cur = start // MOBA_BLOCK
        qb = lax.dynamic_slice_in_dim(q, start, Q_BLOCK, axis=2)
        gate = jnp.einsum('bhqd,bhnd->bhqn', qb, kmean).astype(jnp.float32)
        gate = jnp.where(jnp.arange(nb) < cur, gate, -jnp.inf)
        _, sel = lax.top_k(gate, n_topk)
        sel_valid = jnp.arange(n_topk) < cur
        kg = kb[bidx, hidx, sel]
        vg = vb[bidx, hidx, sel]
        l_sel = jnp.einsum('bhqd,bhqjkd->bhqjk', qb, kg).astype(jnp.float32)
        l_sel = jnp.where(sel_valid[:, None], l_sel, -jnp.inf).reshape(b_, A_HEADS, Q_BLOCK, n_sel)
        ko = lax.dynamic_index_in_dim(kb, cur, axis=2, keepdims=False)
        vo = lax.dynamic_index_in_dim(vb, cur, axis=2, keepdims=False)
        l_own = jnp.einsum('bhqd,bhkd->bhqk', qb, ko).astype(jnp.float32)
        qpos = start + jnp.arange(Q_BLOCK)
        kpos = cur * MOBA_BLOCK + jnp.arange(MOBA_BLOCK)
        l_own = jnp.where(kpos[None, :] <= qpos[:, None], l_own, -jnp.inf)
        p = jax.nn.softmax(jnp.concatenate([l_sel, l_own], axis=-1), axis=-1).astype(vb.dtype)
        p_sel = p[..., :n_sel].reshape(b_, A_HEADS, Q_BLOCK, n_topk, MOBA_BLOCK)
        p_own = p[..., n_sel:]
        return jnp.einsum('bhqjk,bhqjkd->bhqd', p_sel, vg) + jnp.einsum('bhqk,bhkd->bhqd', p_own, vo)

    o = lax.map(attend_block, jnp.arange(s_ // Q_BLOCK))
    o = o.transpose(1, 0, 3, 2, 4).reshape(b_, s_, A_HEADS * A_DH)
    return o @ w_o


def setup_inputs(seed: int = 0) -> dict:
    key = jax.random.key(seed)
    ks = jax.random.split(key, 20)
    f32 = jnp.float32

    def w(k, shape, fan_in, scale=1.0):
        return jax.random.normal(k, shape, f32) * (scale * fan_in ** -0.5)

    def gain(k, shape):
        return 1.0 + 0.05 * jax.random.normal(k, shape, f32)

    res_scale = (2 * DEPTH) ** -0.5
    gate_base = jnp.stack([jnp.zeros((M_HEADS,), f32), jnp.linspace(3.0, 6.0, M_HEADS, dtype=f32)])
    return {
        'x': jax.random.normal(ks[0], (BATCH, SEQ, D_MODEL), f32),
        'a_norm': gain(ks[1], (N_A, D_MODEL)),
        'a_w_in': w(ks[2], (N_A, D_MODEL, M_IN), D_MODEL),
        'a_b_gates': gate_base[None] + 0.1 * jax.random.normal(ks[3], (N_A, 2, M_HEADS), f32),
        'a_h_norm': gain(ks[4], (N_A, M_HEADS * M_DV)),
        'a_w_out': w(ks[5], (N_A, D_MODEL, D_MODEL), D_MODEL, res_scale),
        'kv_norm': gain(ks[6], (D_MODEL,)),
        'w_kv': w(ks[7], (D_MODEL, 2 * D_MODEL), D_MODEL),
        'k_norm': gain(ks[8], (A_DH,)),
        'b_norm': gain(ks[9], (N_B, D_MODEL)),
        'b_w_q': w(ks[10], (N_B, D_MODEL, D_MODEL), D_MODEL),
        'b_q_norm': gain(ks[11], (N_B, A_DH)),
        'b_w_o': w(ks[12], (N_B, D_MODEL, D_MODEL), D_MODEL, res_scale),
        'f_norm': gain(ks[13], (DEPTH, D_MODEL)),
        'f_w_up': w(ks[14], (DEPTH, D_MODEL, 2 * D_FF), D_MODEL),
        'f_conv_w': w(ks[15], (DEPTH, CONV_W, 2 * D_FF), CONV_W),
        'f_conv_b': 0.02 * jax.random.normal(ks[16], (DEPTH, 2 * D_FF), f32),
        'f_w_down': w(ks[17], (DEPTH, D_FF, D_MODEL), D_FF, res_scale),
    }


def reference(x, a_norm, a_w_in, a_b_gates, a_h_norm, a_w_out, kv_norm, w_kv, k_norm,
              b_norm, b_w_q, b_q_norm, b_w_o, f_norm, f_w_up, f_conv_w, f_conv_b, f_w_down):
    cos, sin = rope_tables(x.shape[1])
    h = x
    kb = vb = kmean = None
    for l in range(DEPTH):
        if l < N_A:
            h = h + mlstm_mixer(h, a_norm[l], a_w_in[l], a_b_gates[l], a_h_norm[l], a_w_out[l])
        else:
            j = l - N_A
            h = h + moba_mixer(h, b_norm[j], b_w_q[j], b_q_norm[j], b_w_o[j], kb, vb, kmean, cos, sin)
        h = h + conv_ffn(h, f_norm[l], f_w_up[l], f_conv_w[l], f_conv_b[l], f_w_down[l])
        if l == N_A - 1:
            kb, vb, kmean = shared_kv(h, kv_norm, w_kv, k_norm, cos, sin)
    return h
```

```python
import functools

import jax
import jax.numpy as jnp
from jax import lax
from jax.experimental import pallas as pl
from jax.experimental.pallas import tpu as pltpu

F32 = jnp.float32
BF16 = jnp.bfloat16

D_MODEL = 1024
DEPTH = 4
N_A = DEPTH // 2
M_HEADS = 4
M_DQK = 128
M_DV = D_MODEL // M_HEADS
M_QKVO = 2 * M_HEADS * M_DQK + 2 * M_HEADS * M_DV
A_HEADS = 8
A_DH = D_MODEL // A_HEADS
ROT_DIM = A_DH // 4
ROPE_THETA = 500000.0
MOBA_BLOCK = 256
MOBA_TOPK = 3
D_FF = 2816
CONV_W = 3
EPS = 1e-6

M_CHUNK = 256
ROW_TILE = 512
FFN_CHUNK = 256
VMEM_LIMIT = 56 * 1024 * 1024
NEG_INF = float("-inf")
HIGHEST = lax.Precision.HIGHEST


def _rms_rows(x, gain):
    return x * lax.rsqrt(jnp.mean(x * x, axis=-1, keepdims=True) + EPS) * gain


def _nt_dot(a, b, **kw):
    return lax.dot_general(a, b, (((1,), (1,)), ((), ())), preferred_element_type=F32, **kw)


def _headnorm_rope(t, gain, rc, rs1, rs2, scale):
    outs = []
    for h in range(t.shape[1] // A_DH):
        th = _rms_rows(t[:, h * A_DH:(h + 1) * A_DH], gain)
        th = th * rc + pltpu.roll(th, A_DH - ROT_DIM // 2, 1) * rs1 + pltpu.roll(th, ROT_DIM // 2, 1) * rs2
        outs.append(th * scale if scale != 1.0 else th)
    return jnp.concatenate(outs, axis=1)


def _mlstm_in_kernel(x_ref, g_ref, w_ref, wgt_ref, u_ref, gt_ref, xn_ref):
    @pl.when(pl.program_id(1) == 0)
    def _():
        xn = _rms_rows(x_ref[...], g_ref[...])
        xn_ref[...] = xn.astype(BF16)
        gt_ref[...] = _nt_dot(wgt_ref[...], xn, precision=HIGHEST)
    u_ref[...] = jnp.dot(xn_ref[...], w_ref[...], preferred_element_type=F32).astype(BF16)


def _mlstm_in_proj(x, gain, w_qkvo, w_gates_t):
    s = x.shape[0]
    tn = 1024
    return pl.pallas_call(
        _mlstm_in_kernel,
        grid=(s // ROW_TILE, M_QKVO // tn),
        in_specs=[
            pl.BlockSpec((ROW_TILE, D_MODEL), lambda i, j: (i, 0)),
            pl.BlockSpec((1, D_MODEL), lambda i, j: (0, 0)),
            pl.BlockSpec((D_MODEL, tn), lambda i, j: (0, j)),
            pl.BlockSpec((8, D_MODEL), lambda i, j: (0, 0)),
        ],
        out_specs=[
            pl.BlockSpec((ROW_TILE, tn), lambda i, j: (i, j)),
            pl.BlockSpec((8, ROW_TILE), lambda i, j: (0, i)),
        ],
        out_shape=[jax.ShapeDtypeStruct((s, M_QKVO), BF16), jax.ShapeDtypeStruct((8, s), F32)],
        scratch_shapes=[pltpu.VMEM((ROW_TILE, D_MODEL), BF16)],
        compiler_params=pltpu.CompilerParams(
            dimension_semantics=("parallel", "arbitrary"), vmem_limit_bytes=VMEM_LIMIT),
        name="mlstm_in_proj",
    )(x, gain, w_qkvo, w_gates_t)


def _mlstm_core_kernel(x_ref, q_ref, k_ref, v_ref, o_ref, gt_ref, bias_ref, hg_ref, wout_ref,
                       y_ref, c_ref, n_ref, m_ref):
    L = M_CHUNK

    @pl.when(pl.program_id(0) == 0)
    def _():
        c_ref[...] = jnp.zeros_like(c_ref)
        n_ref[...] = jnp.zeros_like(n_ref)
        m_ref[...] = jnp.zeros_like(m_ref)

    g = gt_ref[...] + bias_ref[...]
    fpre = g[M_HEADS:]
    lf = jnp.minimum(fpre, 0.0) - jnp.log1p(jnp.exp(-jnp.abs(fpre)))
    rows = jnp.concatenate([lf, g[:M_HEADS]], axis=0)
    r_i = lax.broadcasted_iota(jnp.int32, (L, L), 0)
    c_i = lax.broadcasted_iota(jnp.int32, (L, L), 1)
    tri = c_i <= r_i
    tri_l = tri.astype(F32)
    tri_u = (r_i <= c_i).astype(F32)
    cum_rows = jnp.dot(rows, tri_u, preferred_element_type=F32, precision=HIGHEST)
    cum_cols = _nt_dot(tri_l, rows, precision=HIGHEST)

    scale = M_DQK ** -0.5
    heads = []
    for h in range(M_HEADS):
        qh = (q_ref[:, h * M_DQK:(h + 1) * M_DQK].astype(F32) * scale).astype(BF16)
        kh = k_ref[:, h * M_DQK:(h + 1) * M_DQK]
        vh = v_ref[:, h * M_DV:(h + 1) * M_DV]
        b_row = cum_rows[h:h + 1, :]
        b_col = cum_cols[:, h:h + 1]
        i_row = rows[M_HEADS + h:M_HEADS + h + 1, :]
        m_prev = m_ref[h:h + 1, 0:1]
        c_prev = c_ref[h]
        n_prev = n_ref[h]

        d = jnp.where(tri, b_col - b_row + i_row, NEG_INF)
        inter = b_col + m_prev
        m_t = jnp.maximum(inter, jnp.max(d, axis=-1, keepdims=True))
        w_inter = jnp.exp(inter - m_t)
        s_mat = _nt_dot(qh, kh) * jnp.exp(d - m_t)
        num = w_inter * jnp.dot(qh, c_prev.astype(BF16), preferred_element_type=F32) \
            + jnp.dot(s_mat.astype(BF16), vh, preferred_element_type=F32)
        qn = _nt_dot(qh, n_prev.astype(BF16))[:, 0:1]
        den = w_inter * qn + jnp.sum(s_mat, axis=-1, keepdims=True)
        hh = num / jnp.maximum(jnp.abs(den), jnp.exp(-m_t))

        b_last = b_row[:, L - 1:L]
        g_row = b_last - b_row + i_row
        m_new = jnp.maximum(b_last + m_prev, jnp.max(g_row, axis=-1, keepdims=True))
        decay = jnp.exp(b_last + m_prev - m_new)
        w_row = jnp.exp(g_row - m_new)
        wk_t = (kh.astype(F32).T * w_row).astype(BF16)
        c_ref[h] = decay * c_prev + jnp.dot(wk_t, vh, preferred_element_type=F32)
        n_ref[h] = decay * n_prev + jnp.dot(jnp.broadcast_to(w_row, (8, L)).astype(BF16), kh,
                                            preferred_element_type=F32)
        m_ref[h:h + 1, :] = jnp.broadcast_to(m_new, (1, 128))

        hn = _rms_rows(hh, hg_ref[:, h * M_DV:(h + 1) * M_DV])
        heads.append((hn * jax.nn.sigmoid(o_ref[:, h * M_DV:(h + 1) * M_DV].astype(F32))).astype(BF16))

    hcat = jnp.concatenate(heads, axis=1)
    y_ref[...] = x_ref[...] + jnp.dot(hcat, wout_ref[...], preferred_element_type=F32)


def _mlstm_core(x, u, gates_t, bias, h_gain, w_out):
    s = x.shape[0]
    L = M_CHUNK
    return pl.pallas_call(
        _mlstm_core_kernel,
        grid=(s // L,),
        in_specs=[
            pl.BlockSpec((L, D_MODEL), lambda c: (c, 0)),
            pl.BlockSpec((L, M_HEADS * M_DQK), lambda c: (c, 0)),
            pl.BlockSpec((L, M_HEADS * M_DQK), lambda c: (c, 1)),
            pl.BlockSpec((L, M_HEADS * M_DV), lambda c: (c, 1)),
            pl.BlockSpec((L, M_HEADS * M_DV), lambda c: (c, 2)),
            pl.BlockSpec((8, L), lambda c: (0, c)),
            pl.BlockSpec((8, 1), lambda c: (0, 0)),
            pl.BlockSpec((1, D_MODEL), lambda c: (0, 0)),
            pl.BlockSpec((D_MODEL, D_MODEL), lambda c: (0, 0)),
        ],
        out_specs=pl.BlockSpec((L, D_MODEL), lambda c: (c, 0)),
        out_shape=jax.ShapeDtypeStruct((s, D_MODEL), F32),
        scratch_shapes=[
            pltpu.VMEM((M_HEADS, M_DQK, M_DV), F32),
            pltpu.VMEM((M_HEADS, 8, M_DQK), F32),
            pltpu.VMEM((8, 128), F32),
        ],
        compiler_params=pltpu.CompilerParams(
            dimension_semantics=("arbitrary",), vmem_limit_bytes=VMEM_LIMIT),
        name="mlstm_core",
    )(x, u, u, u, u, gates_t, bias, h_gain, w_out)


def _conv_ffn_kernel(x_ref, g_ref, wup_ref, cw_ref, cb_ref, wdown_ref, y_ref, carry_ref, act_ref):
    tm = x_ref.shape[0]

    @pl.when(pl.program_id(0) == 0)
    def _():
        carry_ref[...] = jnp.zeros_like(carry_ref)

    x = x_ref[...]
    xn = _rms_rows(x, g_ref[...]).astype(BF16)
    top = lax.broadcasted_iota(jnp.int32, (8, FFN_CHUNK), 0)

    def conv_cols(col):
        u = jnp.dot(xn, wup_ref[:, col:col + FFN_CHUNK], preferred_element_type=F32)
        prev = carry_ref[:, col:col + FFN_CHUNK]
        u1 = pltpu.roll(u, 1, 0)
        u2 = pltpu.roll(u, 2, 0)
        u1_top = jnp.where(top == 0, prev[7:8], u1[0:8])
        u2_top = jnp.where(top == 0, prev[6:7], jnp.where(top == 1, prev[7:8], u2[0:8]))
        u1 = jnp.concatenate([u1_top, u1[8:]], axis=0)
        u2 = jnp.concatenate([u2_top, u2[8:]], axis=0)
        carry_ref[:, col:col + FFN_CHUNK] = u[tm - 8:tm]
        cw = cw_ref[:, col:col + FFN_CHUNK]
        return cb_ref[:, col:col + FFN_CHUNK] + u2 * cw[0:1] + u1 * cw[1:2] + u * cw[2:3]

    for c in range(D_FF // FFN_CHUNK):
        val = conv_cols(c * FFN_CHUNK)
        gate = conv_cols(D_FF + c * FFN_CHUNK)
        act_ref[:, c * FFN_CHUNK:(c + 1) * FFN_CHUNK] = (gate * jax.nn.sigmoid(gate) * val).astype(BF16)

    y_ref[...] = x + jnp.dot(act_ref[...], wdown_ref[...], preferred_element_type=F32)


def _conv_ffn(x, gain, w_up, conv_w, conv_b, w_down):
    s = x.shape[0]
    const = lambda i: (0, 0)
    return pl.pallas_call(
        _conv_ffn_kernel,
        grid=(s // ROW_TILE,),
        in_specs=[
            pl.BlockSpec((ROW_TILE, D_MODEL), lambda i: (i, 0)),
            pl.BlockSpec((1, D_MODEL), const),
            pl.BlockSpec((D_MODEL, 2 * D_FF), const, pipeline_mode=pl.Buffered(1)),
            pl.BlockSpec((CONV_W, 2 * D_FF), const),
            pl.BlockSpec((1, 2 * D_FF), const),
            pl.BlockSpec((D_FF, D_MODEL), const, pipeline_mode=pl.Buffered(1)),
        ],
        out_specs=pl.BlockSpec((ROW_TILE, D_MODEL), lambda i: (i, 0)),
        out_shape=jax.ShapeDtypeStruct((s, D_MODEL), F32),
        scratch_shapes=[pltpu.VMEM((8, 2 * D_FF), F32), pltpu.VMEM((ROW_TILE, D_FF), BF16)],
        compiler_params=pltpu.CompilerParams(
            dimension_semantics=("arbitrary",), vmem_limit_bytes=VMEM_LIMIT),
        name="conv_ffn",
    )(x, gain, w_up, conv_w, conv_b, w_down)


def _kv_proj_kernel(x_ref, g_ref, w_ref, kg_ref, rc_ref, rs1_ref, rs2_ref, k_ref, vt_ref, km_ref, xn_ref):
    nb = k_ref.shape[0]
    j = pl.program_id(1)

    @pl.when(j == 0)
    def _():
        xn = _rms_rows(x_ref[...], g_ref[...]).astype(BF16)
        xn_ref[...] = xn
        t = jnp.dot(xn, w_ref[...], preferred_element_type=F32)
        k = _headnorm_rope(t, kg_ref[...], rc_ref[...], rs1_ref[...], rs2_ref[...], 1.0)
        k = k.reshape(nb, MOBA_BLOCK, D_MODEL)
        km_ref[...] = jnp.mean(k, axis=1, keepdims=True)
        k_ref[...] = k.astype(BF16)

    @pl.when(j == 1)
    def _():
        vt = jnp.dot(xn_ref[...], w_ref[...], preferred_element_type=F32).T
        for b in range(nb):
            vt_ref[b] = vt[:, b * MOBA_BLOCK:(b + 1) * MOBA_BLOCK].astype(BF16)


def _kv_proj(x, gain, w_kv, k_gain, rc, rs1, rs2):
    s = x.shape[0]
    nb = ROW_TILE // MOBA_BLOCK
    n_blocks = s // MOBA_BLOCK
    return pl.pallas_call(
        _kv_proj_kernel,
        grid=(s // ROW_TILE, 2),
        in_specs=[
            pl.BlockSpec((ROW_TILE, D_MODEL), lambda i, j: (i, 0)),
            pl.BlockSpec((1, D_MODEL), lambda i, j: (0, 0)),
            pl.BlockSpec((D_MODEL, D_MODEL), lambda i, j: (0, j)),
            pl.BlockSpec((1, A_DH), lambda i, j: (0, 0)),
            pl.BlockSpec((ROW_TILE, A_DH), lambda i, j: (i, 0)),
            pl.BlockSpec((ROW_TILE, A_DH), lambda i, j: (i, 0)),
            pl.BlockSpec((ROW_TILE, A_DH), lambda i, j: (i, 0)),
        ],
        out_specs=[
            pl.BlockSpec((nb, MOBA_BLOCK, D_MODEL), lambda i, j: (i, 0, 0)),
            pl.BlockSpec((nb, D_MODEL, MOBA_BLOCK), lambda i, j: (i, 0, 0)),
            pl.BlockSpec((nb, 1, D_MODEL), lambda i, j: (i, 0, 0)),
        ],
        out_shape=[
            jax.ShapeDtypeStruct((n_blocks, MOBA_BLOCK, D_MODEL), BF16),
            jax.ShapeDtypeStruct((n_blocks, D_MODEL, MOBA_BLOCK), BF16),
            jax.ShapeDtypeStruct((n_blocks, 1, D_MODEL), F32),
        ],
        scratch_shapes=[pltpu.VMEM((ROW_TILE, D_MODEL), BF16)],
        compiler_params=pltpu.CompilerParams(
            dimension_semantics=("parallel", "arbitrary"), vmem_limit_bytes=VMEM_LIMIT),
        name="kv_proj",
    )(x, gain, w_kv, k_gain, rc, rs1, rs2)


def _q_proj_kernel(x_ref, g_ref, w_ref, qg_ref, rc_ref, rs1_ref, rs2_ref, qt_ref):
    xn = _rms_rows(x_ref[...], g_ref[...]).astype(BF16)
    t = jnp.dot(xn, w_ref[...], preferred_element_type=F32)
    q = _headnorm_rope(t, qg_ref[...], rc_ref[...], rs1_ref[...], rs2_ref[...], A_DH ** -0.5)
    qt_ref[...] = q.T.astype(BF16)


def _q_proj(x, gain, w_q, q_gain, rc, rs1, rs2):
    s = x.shape[0]
    return pl.pallas_call(
        _q_proj_kernel,
        grid=(s // ROW_TILE,),
        in_specs=[
            pl.BlockSpec((ROW_TILE, D_MODEL), lambda i: (i, 0)),
            pl.BlockSpec((1, D_MODEL), lambda i: (0, 0)),
            pl.BlockSpec((D_MODEL, D_MODEL), lambda i: (0, 0)),
            pl.BlockSpec((1, A_DH), lambda i: (0, 0)),
            pl.BlockSpec((ROW_TILE, A_DH), lambda i: (i, 0)),
            pl.BlockSpec((ROW_TILE, A_DH), lambda i: (i, 0)),
            pl.BlockSpec((ROW_TILE, A_DH), lambda i: (i, 0)),
        ],
        out_specs=pl.BlockSpec((D_MODEL, ROW_TILE), lambda i: (0, i)),
        out_shape=jax.ShapeDtypeStruct((D_MODEL, s), BF16),
        compiler_params=pltpu.CompilerParams(
            dimension_semantics=("parallel",), vmem_limit_bytes=VMEM_LIMIT),
        name="q_proj",
    )(x, gain, w_q, q_gain, rc, rs1, rs2)


def _moba_attn_kernel(qt_ref, k_ref, vt_ref, km_ref, o_ref, sel_ref):
    nb = k_ref.shape[0]
    bs = MOBA_BLOCK
    cur = pl.program_id(1)
    qt = qt_ref[...]

    gate = jnp.dot(km_ref[...], qt, preferred_element_type=F32)
    blk = lax.broadcasted_iota(jnp.int32, gate.shape, 0)
    gate = jnp.where(blk < cur, gate, NEG_INF)
    sel = jnp.zeros(gate.shape, F32)
    for _ in range(MOBA_TOPK):
        top = jnp.max(gate, axis=0, keepdims=True)
        first = jnp.min(jnp.where((gate == top) & (gate > NEG_INF), blk, nb), axis=0, keepdims=True)
        pick = blk == first
        sel = jnp.where(pick, 1.0, sel)
        gate = jnp.where(pick, NEG_INF, gate)
    sel_ref[...] = sel

    s_own = jnp.dot(k_ref[cur], qt, preferred_element_type=F32)
    kpos = lax.broadcasted_iota(jnp.int32, (bs, bs), 0)
    qpos = lax.broadcasted_iota(jnp.int32, (bs, bs), 1)
    s_own = jnp.where(kpos <= qpos, s_own, NEG_INF)
    m0 = jnp.max(s_own, axis=0, keepdims=True)
    p0 = jnp.exp(s_own - m0)
    l0 = jnp.sum(p0, axis=0, keepdims=True)
    acc0 = jnp.dot(vt_ref[cur], p0.astype(BF16), preferred_element_type=F32)

    def body(j, carry):
        m, l, acc = carry
        s_t = jnp.dot(k_ref[j], qt, preferred_element_type=F32)
        s_t = jnp.where(sel_ref[pl.ds(j, 1), :] > 0.0, s_t, NEG_INF)
        m_new = jnp.maximum(m, jnp.max(s_t, axis=0, keepdims=True))
        alpha = jnp.exp(m - m_new)
        p = jnp.exp(s_t - m_new)
        l = alpha * l + jnp.sum(p, axis=0, keepdims=True)
        acc = alpha * acc + jnp.dot(vt_ref[j], p.astype(BF16), preferred_element_type=F32)
        return m_new, l, acc

    _, l, acc = lax.fori_loop(0, cur, body, (m0, l0, acc0))
    o_ref[...] = (acc / l).T.astype(BF16)


def _moba_attn(qt, kb, vtb, kmean):
    s = qt.shape[1]
    nb = s // MOBA_BLOCK
    return pl.pallas_call(
        _moba_attn_kernel,
        grid=(A_HEADS, nb),
        in_specs=[
            pl.BlockSpec((A_DH, MOBA_BLOCK), lambda h, i: (h, i)),
            pl.BlockSpec((nb, MOBA_BLOCK, A_DH), lambda h, i: (0, 0, h)),
            pl.BlockSpec((nb, A_DH, MOBA_BLOCK), lambda h, i: (0, h, 0)),
            pl.BlockSpec((nb, A_DH), lambda h, i: (0, h)),
        ],
        out_specs=pl.BlockSpec((MOBA_BLOCK, A_DH), lambda h, i: (i, h)),
        out_shape=jax.ShapeDtypeStruct((s, D_MODEL), BF16),
        scratch_shapes=[pltpu.VMEM((nb, MOBA_BLOCK), F32)],
        compiler_params=pltpu.CompilerParams(
            dimension_semantics=("parallel", "arbitrary"), vmem_limit_bytes=VMEM_LIMIT),
        name="moba_attn",
    )(qt, kb, vtb, kmean)


def _proj_residual_kernel(x_ref, a_ref, w_ref, y_ref):
    y_ref[...] = x_ref[...] + jnp.dot(a_ref[...], w_ref[...], preferred_element_type=F32)


def _proj_residual(x, a, w):
    s = x.shape[0]
    return pl.pallas_call(
        _proj_residual_kernel,
        grid=(s // ROW_TILE,),
        in_specs=[
            pl.BlockSpec((ROW_TILE, D_MODEL), lambda i: (i, 0)),
            pl.BlockSpec((ROW_TILE, D_MODEL), lambda i: (i, 0)),
            pl.BlockSpec((D_MODEL, D_MODEL), lambda i: (0, 0)),
        ],
        out_specs=pl.BlockSpec((ROW_TILE, D_MODEL), lambda i: (i, 0)),
        out_shape=jax.ShapeDtypeStruct((s, D_MODEL), F32),
        compiler_params=pltpu.CompilerParams(
            dimension_semantics=("parallel",), vmem_limit_bytes=VMEM_LIMIT),
        name="proj_residual",
    )(x, a, w)


def _rope_lane_tables(s_len):
    pos = jnp.arange(s_len, dtype=F32)
    inv = ROPE_THETA ** (-jnp.arange(0, ROT_DIM, 2, dtype=F32) / ROT_DIM)
    ang = pos[:, None] * inv[None, :]
    cos, sin = jnp.cos(ang), jnp.sin(ang)
    half = ROT_DIM // 2
    rc = jnp.concatenate([cos, cos, jnp.ones((s_len, A_DH - ROT_DIM), F32)], axis=1)
    rs1 = jnp.concatenate([-sin, jnp.zeros((s_len, A_DH - half), F32)], axis=1)
    rs2 = jnp.concatenate([jnp.zeros((s_len, half), F32), sin, jnp.zeros((s_len, A_DH - ROT_DIM), F32)], axis=1)
    return rc, rs1, rs2


def kernel(x, a_norm, a_w_in, a_b_gates, a_h_norm, a_w_out, kv_norm, w_kv, k_norm, b_norm, b_w_q, b_q_norm,
           b_w_o, f_norm, f_w_up, f_conv_w, f_conv_b, f_w_down):
    b_, s_, d_ = x.shape
    assert (b_, d_) == (1, D_MODEL) and s_ % ROW_TILE == 0 and s_ % MOBA_BLOCK == 0
    rc, rs1, rs2 = _rope_lane_tables(s_)
    h = x.reshape(s_, d_)
    kb = vtb = kmean = None
    for l in range(DEPTH):
        if l < N_A:
            w_in = a_w_in[l]
            u, gates_t = _mlstm_in_proj(h, a_norm[l][None], w_in[:, :M_QKVO].astype(BF16), w_in[:, M_QKVO:].T)
            h = _mlstm_core(h, u, gates_t, a_b_gates[l].reshape(2 * M_HEADS, 1), a_h_norm[l][None],
                            a_w_out[l].astype(BF16))
        else:
            j = l - N_A
            qt = _q_proj(h, b_norm[j][None], b_w_q[j].astype(BF16), b_q_norm[j][None], rc, rs1, rs2)
            o = _moba_attn(qt, kb, vtb, kmean)
            h = _proj_residual(h, o, b_w_o[j].astype(BF16))
        h = _conv_ffn(h, f_norm[l][None], f_w_up[l].astype(BF16), f_conv_w[l], f_conv_b[l][None],
                      f_w_down[l].astype(BF16))
        if l == N_A - 1:
            kb, vtb, km = _kv_proj(h, kv_norm[None], w_kv.astype(BF16), k_norm[None], rc, rs1, rs2)
            kmean = km.reshape(s_ // MOBA_BLOCK, D_MODEL).astype(BF16)
    return h.reshape(b_, s_, d_)
```

```python
import functools

import jax
import jax.numpy as jnp
from jax import lax
from jax.experimental import pallas as pl
from jax.experimental.pallas import tpu as pltpu

F32 = jnp.float32
BF16 = jnp.bfloat16

D_MODEL = 1024
DEPTH = 4
N_A = DEPTH // 2
M_HEADS = 4
M_DQK = 128
M_DV = D_MODEL // M_HEADS
M_QKVO = 2 * M_HEADS * M_DQK + 2 * M_HEADS * M_DV
A_HEADS = 8
A_DH = D_MODEL // A_HEADS
ROT_DIM = A_DH // 4
ROPE_THETA = 500000.0
MOBA_BLOCK = 256
MOBA_TOPK = 3
D_FF = 2816
CONV_W = 3
EPS = 1e-6

M_CHUNK = 256
ROW_TILE = 512
FFN_CHUNK = 256
ATTN_HEADS = 2
ATTN_UNROLL = 2
LOG2_E = 1.4426950408889634
VMEM_LIMIT = 56 * 1024 * 1024
NEG_INF = float("-inf")
HIGHEST = lax.Precision.HIGHEST


def _rms_rows(x, gain):
    return x * lax.rsqrt(jnp.mean(x * x, axis=-1, keepdims=True) + EPS) * gain


def _nt_dot(a, b, **kw):
    return lax.dot_general(a, b, (((1,), (1,)), ((), ())), preferred_element_type=F32, **kw)


def _headnorm_rope(t, gain, rc, rs1, rs2, scale):
    outs = []
    for h in range(t.shape[1] // A_DH):
        th = _rms_rows(t[:, h * A_DH:(h + 1) * A_DH], gain)
        th = th * rc + pltpu.roll(th, A_DH - ROT_DIM // 2, 1) * rs1 + pltpu.roll(th, ROT_DIM // 2, 1) * rs2
        outs.append(th * scale if scale != 1.0 else th)
    return jnp.concatenate(outs, axis=1)


def _mlstm_in_kernel(x_ref, g_ref, w_ref, wgt_ref, u_ref, gt_ref, xn_ref):
    @pl.when(pl.program_id(1) == 0)
    def _():
        xn = _rms_rows(x_ref[...], g_ref[...])
        xn_ref[...] = xn.astype(BF16)
        gt_ref[...] = _nt_dot(wgt_ref[...], xn, precision=HIGHEST)
    u_ref[...] = jnp.dot(xn_ref[...], w_ref[...], preferred_element_type=F32).astype(BF16)


def _mlstm_in_proj(x, gain, w_qkvo, w_gates_t):
    s = x.shape[0]
    tn = 1024
    return pl.pallas_call(
        _mlstm_in_kernel,
        grid=(s // ROW_TILE, M_QKVO // tn),
        in_specs=[
            pl.BlockSpec((ROW_TILE, D_MODEL), lambda i, j: (i, 0)),
            pl.BlockSpec((1, D_MODEL), lambda i, j: (0, 0)),
            pl.BlockSpec((D_MODEL, tn), lambda i, j: (0, j)),
            pl.BlockSpec((8, D_MODEL), lambda i, j: (0, 0)),
        ],
        out_specs=[
            pl.BlockSpec((ROW_TILE, tn), lambda i, j: (i, j)),
            pl.BlockSpec((8, ROW_TILE), lambda i, j: (0, i)),
        ],
        out_shape=[jax.ShapeDtypeStruct((s, M_QKVO), BF16), jax.ShapeDtypeStruct((8, s), F32)],
        scratch_shapes=[pltpu.VMEM((ROW_TILE, D_MODEL), BF16)],
        compiler_params=pltpu.CompilerParams(
            dimension_semantics=("parallel", "arbitrary"), vmem_limit_bytes=VMEM_LIMIT),
        name="mlstm_in_proj",
    )(x, gain, w_qkvo, w_gates_t)


def _mlstm_core_kernel(x_ref, q_ref, k_ref, v_ref, o_ref, gt_ref, bias_ref, hg_ref, wout_ref,
                       y_ref, c_ref, n_ref, m_ref):
    L = M_CHUNK

    @pl.when(pl.program_id(0) == 0)
    def _():
        c_ref[...] = jnp.zeros_like(c_ref)
        n_ref[...] = jnp.zeros_like(n_ref)
        m_ref[...] = jnp.zeros_like(m_ref)

    g = gt_ref[...] + bias_ref[...]
    fpre = g[M_HEADS:]
    lf = jnp.minimum(fpre, 0.0) - jnp.log1p(jnp.exp(-jnp.abs(fpre)))
    rows = jnp.concatenate([lf, g[:M_HEADS]], axis=0)
    r_i = lax.broadcasted_iota(jnp.int32, (L, L), 0)
    c_i = lax.broadcasted_iota(jnp.int32, (L, L), 1)
    tri = c_i <= r_i
    tri_l = tri.astype(F32)
    tri_u = (r_i <= c_i).astype(F32)
    cum_rows = jnp.dot(rows, tri_u, preferred_element_type=F32, precision=HIGHEST)
    cum_cols = _nt_dot(tri_l, rows, precision=HIGHEST)

    scale = M_DQK ** -0.5
    heads = []
    for h in range(M_HEADS):
        qh = (q_ref[:, h * M_DQK:(h + 1) * M_DQK].astype(F32) * scale).astype(BF16)
        kh = k_ref[:, h * M_DQK:(h + 1) * M_DQK]
        vh = v_ref[:, h * M_DV:(h + 1) * M_DV]
        b_row = cum_rows[h:h + 1, :]
        b_col = cum_cols[:, h:h + 1]
        i_row = rows[M_HEADS + h:M_HEADS + h + 1, :]
        m_prev = m_ref[h:h + 1, 0:1]
        c_prev = c_ref[h]
        n_prev = n_ref[h]

        d = jnp.where(tri, b_col - b_row + i_row, NEG_INF)
        inter = b_col + m_prev
        m_t = jnp.maximum(inter, jnp.max(d, axis=-1, keepdims=True))
        w_inter = jnp.exp(inter - m_t)
        s_mat = _nt_dot(qh, kh) * jnp.exp(d - m_t)
        num = w_inter * jnp.dot(qh, c_prev.astype(BF16), preferred_element_type=F32) \
            + jnp.dot(s_mat.astype(BF16), vh, preferred_element_type=F32)
        qn = _nt_dot(qh, n_prev.astype(BF16))[:, 0:1]
        den = w_inter * qn + jnp.sum(s_mat, axis=-1, keepdims=True)
        hh = num / jnp.maximum(jnp.abs(den), jnp.exp(-m_t))

        b_last = b_row[:, L - 1:L]
        g_row = b_last - b_row + i_row
        m_new = jnp.maximum(b_last + m_prev, jnp.max(g_row, axis=-1, keepdims=True))
        decay = jnp.exp(b_last + m_prev - m_new)
        w_row = jnp.exp(g_row - m_new)
        wk_t = (kh.astype(F32).T * w_row).astype(BF16)
        c_ref[h] = decay * c_prev + jnp.dot(wk_t, vh, preferred_element_type=F32)
        n_ref[h] = decay * n_prev + jnp.dot(jnp.broadcast_to(w_row, (8, L)).astype(BF16), kh,
                                            preferred_element_type=F32)
        m_ref[h:h + 1, :] = jnp.broadcast_to(m_new, (1, 128))

        hn = _rms_rows(hh, hg_ref[:, h * M_DV:(h + 1) * M_DV])
        heads.append((hn * jax.nn.sigmoid(o_ref[:, h * M_DV:(h + 1) * M_DV].astype(F32))).astype(BF16))

    hcat = jnp.concatenate(heads, axis=1)
    y_ref[...] = x_ref[...] + jnp.dot(hcat, wout_ref[...], preferred_element_type=F32)


def _mlstm_core(x, u, gates_t, bias, h_gain, w_out):
    s = x.shape[0]
    L = M_CHUNK
    return pl.pallas_call(
        _mlstm_core_kernel,
        grid=(s // L,),
        in_specs=[
            pl.BlockSpec((L, D_MODEL), lambda c: (c, 0)),
            pl.BlockSpec((L, M_HEADS * M_DQK), lambda c: (c, 0)),
            pl.BlockSpec((L, M_HEADS * M_DQK), lambda c: (c, 1)),
            pl.BlockSpec((L, M_HEADS * M_DV), lambda c: (c, 1)),
            pl.BlockSpec((L, M_HEADS * M_DV), lambda c: (c, 2)),
            pl.BlockSpec((8, L), lambda c: (0, c)),
            pl.BlockSpec((8, 1), lambda c: (0, 0)),
            pl.BlockSpec((1, D_MODEL), lambda c: (0, 0)),
            pl.BlockSpec((D_MODEL, D_MODEL), lambda c: (0, 0)),
        ],
        out_specs=pl.BlockSpec((L, D_MODEL), lambda c: (c, 0)),
        out_shape=jax.ShapeDtypeStruct((s, D_MODEL), F32),
        scratch_shapes=[
            pltpu.VMEM((M_HEADS, M_DQK, M_DV), F32),
            pltpu.VMEM((M_HEADS, 8, M_DQK), F32),
            pltpu.VMEM((8, 128), F32),
        ],
        compiler_params=pltpu.CompilerParams(
            dimension_semantics=("arbitrary",), vmem_limit_bytes=VMEM_LIMIT),
        name="mlstm_core",
    )(x, u, u, u, u, gates_t, bias, h_gain, w_out)


def _conv_ffn_kernel(x_ref, g_ref, wup_ref, cw_ref, cb_ref, wdown_ref, y_ref, carry_ref, act_ref):
    tm = x_ref.shape[0]

    @pl.when(pl.program_id(0) == 0)
    def _():
        carry_ref[...] = jnp.zeros_like(carry_ref)

    x = x_ref[...]
    xn = _rms_rows(x, g_ref[...]).astype(BF16)
    top = lax.broadcasted_iota(jnp.int32, (8, FFN_CHUNK), 0)

    def conv_cols(col):
        u = jnp.dot(xn, wup_ref[:, col:col + FFN_CHUNK], preferred_element_type=F32)
        prev = carry_ref[:, col:col + FFN_CHUNK]
        u1 = pltpu.roll(u, 1, 0)
        u2 = pltpu.roll(u, 2, 0)
        u1_top = jnp.where(top == 0, prev[7:8], u1[0:8])
        u2_top = jnp.where(top == 0, prev[6:7], jnp.where(top == 1, prev[7:8], u2[0:8]))
        u1 = jnp.concatenate([u1_top, u1[8:]], axis=0)
        u2 = jnp.concatenate([u2_top, u2[8:]], axis=0)
        carry_ref[:, col:col + FFN_CHUNK] = u[tm - 8:tm]
        cw = cw_ref[:, col:col + FFN_CHUNK]
        return cb_ref[:, col:col + FFN_CHUNK] + u2 * cw[0:1] + u1 * cw[1:2] + u * cw[2:3]

    for c in range(D_FF // FFN_CHUNK):
        val = conv_cols(c * FFN_CHUNK)
        gate = conv_cols(D_FF + c * FFN_CHUNK)
        act_ref[:, c * FFN_CHUNK:(c + 1) * FFN_CHUNK] = (gate * jax.nn.sigmoid(gate) * val).astype(BF16)

    y_ref[...] = x + jnp.dot(act_ref[...], wdown_ref[...], preferred_element_type=F32)


def _conv_ffn(x, gain, w_up, conv_w, conv_b, w_down):
    s = x.shape[0]
    const = lambda i: (0, 0)
    return pl.pallas_call(
        _conv_ffn_kernel,
        grid=(s // ROW_TILE,),
        in_specs=[
            pl.BlockSpec((ROW_TILE, D_MODEL), lambda i: (i, 0)),
            pl.BlockSpec((1, D_MODEL), const),
            pl.BlockSpec((D_MODEL, 2 * D_FF), const, pipeline_mode=pl.Buffered(1)),
            pl.BlockSpec((CONV_W, 2 * D_FF), const),
            pl.BlockSpec((1, 2 * D_FF), const),
            pl.BlockSpec((D_FF, D_MODEL), const, pipeline_mode=pl.Buffered(1)),
        ],
        out_specs=pl.BlockSpec((ROW_TILE, D_MODEL), lambda i: (i, 0)),
        out_shape=jax.ShapeDtypeStruct((s, D_MODEL), F32),
        scratch_shapes=[pltpu.VMEM((8, 2 * D_FF), F32), pltpu.VMEM((ROW_TILE, D_FF), BF16)],
        compiler_params=pltpu.CompilerParams(
            dimension_semantics=("arbitrary",), vmem_limit_bytes=VMEM_LIMIT),
        name="conv_ffn",
    )(x, gain, w_up, conv_w, conv_b, w_down)


def _kv_proj_kernel(x_ref, g_ref, w_ref, kg_ref, rc_ref, rs1_ref, rs2_ref, k_ref, vt_ref, km_ref, xn_ref):
    nb = k_ref.shape[0]
    j = pl.program_id(1)

    @pl.when(j == 0)
    def _():
        xn = _rms_rows(x_ref[...], g_ref[...]).astype(BF16)
        xn_ref[...] = xn
        t = jnp.dot(xn, w_ref[...], preferred_element_type=F32)
        k = _headnorm_rope(t, kg_ref[...], rc_ref[...], rs1_ref[...], rs2_ref[...], 1.0)
        k = k.reshape(nb, MOBA_BLOCK, D_MODEL)
        km_ref[...] = jnp.mean(k, axis=1, keepdims=True)
        k_ref[...] = k.astype(BF16)

    @pl.when(j == 1)
    def _():
        vt = jnp.dot(xn_ref[...], w_ref[...], preferred_element_type=F32).T
        for b in range(nb):
            vt_ref[b] = vt[:, b * MOBA_BLOCK:(b + 1) * MOBA_BLOCK].astype(BF16)


def _kv_proj(x, gain, w_kv, k_gain, rc, rs1, rs2):
    s = x.shape[0]
    nb = ROW_TILE // MOBA_BLOCK
    n_blocks = s // MOBA_BLOCK
    return pl.pallas_call(
        _kv_proj_kernel,
        grid=(s // ROW_TILE, 2),
        in_specs=[
            pl.BlockSpec((ROW_TILE, D_MODEL), lambda i, j: (i, 0)),
            pl.BlockSpec((1, D_MODEL), lambda i, j: (0, 0)),
            pl.BlockSpec((D_MODEL, D_MODEL), lambda i, j: (0, j)),
            pl.BlockSpec((1, A_DH), lambda i, j: (0, 0)),
            pl.BlockSpec((ROW_TILE, A_DH), lambda i, j: (i, 0)),
            pl.BlockSpec((ROW_TILE, A_DH), lambda i, j: (i, 0)),
            pl.BlockSpec((ROW_TILE, A_DH), lambda i, j: (i, 0)),
        ],
        out_specs=[
            pl.BlockSpec((nb, MOBA_BLOCK, D_MODEL), lambda i, j: (i, 0, 0)),
            pl.BlockSpec((nb, D_MODEL, MOBA_BLOCK), lambda i, j: (i, 0, 0)),
            pl.BlockSpec((nb, 1, D_MODEL), lambda i, j: (i, 0, 0)),
        ],
        out_shape=[
            jax.ShapeDtypeStruct((n_blocks, MOBA_BLOCK, D_MODEL), BF16),
            jax.ShapeDtypeStruct((n_blocks, D_MODEL, MOBA_BLOCK), BF16),
            jax.ShapeDtypeStruct((n_blocks, 1, D_MODEL), F32),
        ],
        scratch_shapes=[pltpu.VMEM((ROW_TILE, D_MODEL), BF16)],
        compiler_params=pltpu.CompilerParams(
            dimension_semantics=("parallel", "arbitrary"), vmem_limit_bytes=VMEM_LIMIT),
        name="kv_proj",
    )(x, gain, w_kv, k_gain, rc, rs1, rs2)


def _q_proj_kernel(x_ref, g_ref, w_ref, qg_ref, rc_ref, rs1_ref, rs2_ref, qt_ref):
    xn = _rms_rows(x_ref[...], g_ref[...]).astype(BF16)
    t = jnp.dot(xn, w_ref[...], preferred_element_type=F32)
    q = _headnorm_rope(t, qg_ref[...], rc_ref[...], rs1_ref[...], rs2_ref[...], A_DH ** -0.5 * LOG2_E)
    qt_ref[...] = q.T.astype(BF16)


def _q_proj(x, gain, w_q, q_gain, rc, rs1, rs2):
    s = x.shape[0]
    return pl.pallas_call(
        _q_proj_kernel,
        grid=(s // ROW_TILE,),
        in_specs=[
            pl.BlockSpec((ROW_TILE, D_MODEL), lambda i: (i, 0)),
            pl.BlockSpec((1, D_MODEL), lambda i: (0, 0)),
            pl.BlockSpec((D_MODEL, D_MODEL), lambda i: (0, 0)),
            pl.BlockSpec((1, A_DH), lambda i: (0, 0)),
            pl.BlockSpec((ROW_TILE, A_DH), lambda i: (i, 0)),
            pl.BlockSpec((ROW_TILE, A_DH), lambda i: (i, 0)),
            pl.BlockSpec((ROW_TILE, A_DH), lambda i: (i, 0)),
        ],
        out_specs=pl.BlockSpec((D_MODEL, ROW_TILE), lambda i: (0, i)),
        out_shape=jax.ShapeDtypeStruct((D_MODEL, s), BF16),
        compiler_params=pltpu.CompilerParams(
            dimension_semantics=("parallel",), vmem_limit_bytes=VMEM_LIMIT),
        name="q_proj",
    )(x, gain, w_q, q_gain, rc, rs1, rs2)


def _moba_attn_kernel(qt_ref, k_ref, vt_ref, km_ref, o_ref, sel_ref):
    nb = k_ref.shape[0]
    bs = MOBA_BLOCK
    cur = pl.program_id(1)
    kpos = lax.broadcasted_iota(jnp.int32, (bs, bs), 0)
    qpos = lax.broadcasted_iota(jnp.int32, (bs, bs), 1)
    causal = kpos <= qpos

    qts, init = [], []
    for hh in range(ATTN_HEADS):
        hs = slice(hh * A_DH, (hh + 1) * A_DH)
        qt = qt_ref[hs, :]
        qts.append(qt)
        gate = jnp.dot(km_ref[:, hs], qt, preferred_element_type=F32)
        blk = lax.broadcasted_iota(jnp.int32, gate.shape, 0)
        gate = jnp.where(blk < cur, gate, NEG_INF)
        sel = jnp.zeros(gate.shape, F32)
        for _ in range(MOBA_TOPK):
            top = jnp.max(gate, axis=0, keepdims=True)
            first = jnp.min(jnp.where((gate == top) & (gate > NEG_INF), blk, nb), axis=0, keepdims=True)
            pick = blk == first
            sel = jnp.where(pick, 1.0, sel)
            gate = jnp.where(pick, NEG_INF, gate)
        sel_ref[hh] = sel

        s_own = jnp.dot(k_ref[cur, :, hs], qt, preferred_element_type=F32)
        s_own = jnp.where(causal, s_own, NEG_INF)
        m0 = jnp.max(s_own, axis=0, keepdims=True)
        p0 = jnp.exp2(s_own - m0)
        l0 = jnp.sum(p0, axis=0, keepdims=True)
        acc0 = jnp.dot(vt_ref[cur, hs, :], p0.astype(BF16), preferred_element_type=F32)
        init.append((m0, l0, acc0))

    def body(step, carry):
        j0 = step * ATTN_UNROLL
        out = []
        for hh in range(ATTN_HEADS):
            hs = slice(hh * A_DH, (hh + 1) * A_DH)
            m, l, acc = carry[hh]
            s_parts, vt_parts = [], []
            for b in range(ATTN_UNROLL):
                s_b = jnp.dot(k_ref[j0 + b, :, hs], qts[hh], preferred_element_type=F32)
                s_parts.append(jnp.where(sel_ref[hh, pl.ds(j0 + b, 1), :] > 0.0, s_b, NEG_INF))
                vt_parts.append(vt_ref[j0 + b, hs, :])
            s_t = jnp.concatenate(s_parts, axis=0)
            m_new = jnp.maximum(m, jnp.max(s_t, axis=0, keepdims=True))
            alpha = jnp.exp2(m - m_new)
            p = jnp.exp2(s_t - m_new)
            l = alpha * l + jnp.sum(p, axis=0, keepdims=True)
            acc = alpha * acc + jnp.dot(jnp.concatenate(vt_parts, axis=1), p.astype(BF16),
                                        preferred_element_type=F32)
            out.append((m_new, l, acc))
        return tuple(out)

    steps = (cur + ATTN_UNROLL - 1) // ATTN_UNROLL
    final = lax.fori_loop(0, steps, body, tuple(init))
    o_ref[...] = jnp.concatenate([(acc / l).T for _, l, acc in final], axis=1).astype(BF16)


def _moba_attn(qt, kb, vtb, kmean):
    s = qt.shape[1]
    nb = s // MOBA_BLOCK
    gw = ATTN_HEADS * A_DH
    return pl.pallas_call(
        _moba_attn_kernel,
        grid=(A_HEADS // ATTN_HEADS, nb),
        in_specs=[
            pl.BlockSpec((gw, MOBA_BLOCK), lambda g, i: (g, i)),
            pl.BlockSpec((nb, MOBA_BLOCK, gw), lambda g, i: (0, 0, g)),
            pl.BlockSpec((nb, gw, MOBA_BLOCK), lambda g, i: (0, g, 0)),
            pl.BlockSpec((nb, gw), lambda g, i: (0, g)),
        ],
        out_specs=pl.BlockSpec((MOBA_BLOCK, gw), lambda g, i: (i, g)),
        out_shape=jax.ShapeDtypeStruct((s, D_MODEL), BF16),
        scratch_shapes=[pltpu.VMEM((ATTN_HEADS, nb, MOBA_BLOCK), F32)],
        compiler_params=pltpu.CompilerParams(
            dimension_semantics=("parallel", "arbitrary"), vmem_limit_bytes=VMEM_LIMIT),
        name="moba_attn",
    )(qt, kb, vtb, kmean)


def _proj_residual_kernel(x_ref, a_ref, w_ref, y_ref):
    y_ref[...] = x_ref[...] + jnp.dot(a_ref[...], w_ref[...], preferred_element_type=F32)


def _proj_residual(x, a, w):
    s = x.shape[0]
    return pl.pallas_call(
        _proj_residual_kernel,
        grid=(s // ROW_TILE,),
        in_specs=[
            pl.BlockSpec((ROW_TILE, D_MODEL), lambda i: (i, 0)),
            pl.BlockSpec((ROW_TILE, D_MODEL), lambda i: (i, 0)),
            pl.BlockSpec((D_MODEL, D_MODEL), lambda i: (0, 0)),
        ],
        out_specs=pl.BlockSpec((ROW_TILE, D_MODEL), lambda i: (i, 0)),
        out_shape=jax.ShapeDtypeStruct((s, D_MODEL), F32),
        compiler_params=pltpu.CompilerParams(
            dimension_semantics=("parallel",), vmem_limit_bytes=VMEM_LIMIT),
        name="proj_residual",
    )(x, a, w)


def _rope_lane_tables(s_len):
    pos = jnp.arange(s_len, dtype=F32)
    inv = ROPE_THETA ** (-jnp.arange(0, ROT_DIM, 2, dtype=F32) / ROT_DIM)
    ang = pos[:, None] * inv[None, :]
    cos, sin = jnp.cos(ang), jnp.sin(ang)
    half = ROT_DIM // 2
    rc = jnp.concatenate([cos, cos, jnp.ones((s_len, A_DH - ROT_DIM), F32)], axis=1)
    rs1 = jnp.concatenate([-sin, jnp.zeros((s_len, A_DH - half), F32)], axis=1)
    rs2 = jnp.concatenate([jnp.zeros((s_len, half), F32), sin, jnp.zeros((s_len, A_DH - ROT_DIM), F32)], axis=1)
    return rc, rs1, rs2


def kernel(x, a_norm, a_w_in, a_b_gates, a_h_norm, a_w_out, kv_norm, w_kv, k_norm, b_norm, b_w_q, b_q_norm,
           b_w_o, f_norm, f_w_up, f_conv_w, f_conv_b, f_w_down):
    b_, s_, d_ = x.shape
    assert (b_, d_) == (1, D_MODEL) and s_ % ROW_TILE == 0 and s_ % MOBA_BLOCK == 0
    rc, rs1, rs2 = _rope_lane_tables(s_)
    h = x.reshape(s_, d_)
    kb = vtb = kmean = None
    for l in range(DEPTH):
        if l < N_A:
            w_in = a_w_in[l]
            u, gates_t = _mlstm_in_proj(h, a_norm[l][None], w_in[:, :M_QKVO].astype(BF16), w_in[:, M_QKVO:].T)
            h = _mlstm_core(h, u, gates_t, a_b_gates[l].reshape(2 * M_HEADS, 1), a_h_norm[l][None],
                            a_w_out[l].astype(BF16))
        else:
            j = l - N_A
            qt = _q_proj(h, b_norm[j][None], b_w_q[j].astype(BF16), b_q_norm[j][None], rc, rs1, rs2)
            o = _moba_attn(qt, kb, vtb, kmean)
            h = _proj_residual(h, o, b_w_o[j].astype(BF16))
        h = _conv_ffn(h, f_norm[l][None], f_w_up[l].astype(BF16), f_conv_w[l], f_conv_b[l][None],
                      f_w_down[l].astype(BF16))
        if l == N_A - 1:
            kb, vtb, km = _kv_proj(h, kv_norm[None], w_kv.astype(BF16), k_norm[None], rc, rs1, rs2)
            kmean = km.reshape(s_ // MOBA_BLOCK, D_MODEL).astype(BF16)
    return h.reshape(b_, s_, d_)
```

```python
import functools

import jax
import jax.numpy as jnp
from jax import lax
from jax.experimental import pallas as pl
from jax.experimental.pallas import tpu as pltpu

F32 = jnp.float32
BF16 = jnp.bfloat16

D_MODEL = 1024
DEPTH = 4
N_A = DEPTH // 2
M_HEADS = 4
M_DQK = 128
M_DV = D_MODEL // M_HEADS
M_QKVO = 2 * M_HEADS * M_DQK + 2 * M_HEADS * M_DV
A_HEADS = 8
A_DH = D_MODEL // A_HEADS
ROT_DIM = A_DH // 4
ROPE_THETA = 500000.0
MOBA_BLOCK = 256
MOBA_TOPK = 3
D_FF = 2816
CONV_W = 3
EPS = 1e-6

M_CHUNK = 256
ROW_TILE = 512
FFN_CHUNK = 256
ATTN_HEADS = 4
ATTN_UNROLL = 2
FAST_UNROLL = 4
ONES_ROWS = 16
LOG2_E = 1.4426950408889634
MASKED_SCORE = -1e30
MAX_EXP2_ARG = 100.0
NORM_BOUND_SLACK = 1.0 + 2.0 ** -6
VMEM_LIMIT = 56 * 1024 * 1024
NEG_INF = float("-inf")
HIGHEST = lax.Precision.HIGHEST


def _rms_rows(x, gain):
    return x * lax.rsqrt(jnp.mean(x * x, axis=-1, keepdims=True) + EPS) * gain


def _nt_dot(a, b, **kw):
    return lax.dot_general(a, b, (((1,), (1,)), ((), ())), preferred_element_type=F32, **kw)


def _headnorm_rope(t, gain, rc, rs1, rs2, scale):
    outs = []
    for h in range(t.shape[1] // A_DH):
        th = _rms_rows(t[:, h * A_DH:(h + 1) * A_DH], gain)
        th = th * rc + pltpu.roll(th, A_DH - ROT_DIM // 2, 1) * rs1 + pltpu.roll(th, ROT_DIM // 2, 1) * rs2
        outs.append(th * scale if scale != 1.0 else th)
    return jnp.concatenate(outs, axis=1)


def _mlstm_in_kernel(x_ref, g_ref, w_ref, wg_ref, u_ref, gt_ref):
    xn = _rms_rows(x_ref[...], g_ref[...])
    xh = xn.astype(BF16)
    u_ref[...] = jnp.dot(xh, w_ref[...], preferred_element_type=F32).astype(BF16)
    xl = (xn - xh.astype(F32)).astype(BF16)
    wg = wg_ref[...]
    e = jnp.dot(xh, wg, preferred_element_type=F32) + jnp.dot(xl, wg, preferred_element_type=F32)
    et = e.T
    gt_ref[...] = et[0:2 * M_HEADS] + et[2 * M_HEADS:4 * M_HEADS]


def _mlstm_in_proj(x, gain, w_qkvo, w_gates):
    s = x.shape[0]
    return pl.pallas_call(
        _mlstm_in_kernel,
        grid=(s // ROW_TILE,),
        in_specs=[
            pl.BlockSpec((ROW_TILE, D_MODEL), lambda i: (i, 0)),
            pl.BlockSpec((1, D_MODEL), lambda i: (0, 0)),
            pl.BlockSpec((D_MODEL, M_QKVO), lambda i: (0, 0), pipeline_mode=pl.Buffered(1)),
            pl.BlockSpec((D_MODEL, 128), lambda i: (0, 0)),
        ],
        out_specs=[
            pl.BlockSpec((ROW_TILE, M_QKVO), lambda i: (i, 0)),
            pl.BlockSpec((2 * M_HEADS, ROW_TILE), lambda i: (0, i)),
        ],
        out_shape=[jax.ShapeDtypeStruct((s, M_QKVO), BF16), jax.ShapeDtypeStruct((2 * M_HEADS, s), F32)],
        compiler_params=pltpu.CompilerParams(
            dimension_semantics=("parallel",), vmem_limit_bytes=VMEM_LIMIT),
        name="mlstm_in_proj",
    )(x, gain, w_qkvo, w_gates)


def _mlstm_core_kernel(x_ref, q_ref, k_ref, v_ref, o_ref, gt_ref, bias_ref, hg_ref, wout_ref,
                       y_ref, c_ref, n_ref, m_ref):
    L = M_CHUNK

    @pl.when(pl.program_id(0) == 0)
    def _():
        c_ref[...] = jnp.zeros_like(c_ref)
        n_ref[...] = jnp.zeros_like(n_ref)
        m_ref[...] = jnp.zeros_like(m_ref)

    g = gt_ref[...] + bias_ref[...]
    fpre = g[M_HEADS:]
    lf = jnp.minimum(fpre, 0.0) - jnp.log1p(jnp.exp(-jnp.abs(fpre)))
    rows = jnp.concatenate([lf, g[:M_HEADS]], axis=0)
    r_i = lax.broadcasted_iota(jnp.int32, (L, L), 0)
    c_i = lax.broadcasted_iota(jnp.int32, (L, L), 1)
    tri = c_i <= r_i
    tri_l = tri.astype(F32)
    tri_u = (r_i <= c_i).astype(F32)
    cum_rows = jnp.dot(rows, tri_u, preferred_element_type=F32, precision=HIGHEST)
    cum_cols = _nt_dot(tri_l, rows, precision=HIGHEST)

    scale = M_DQK ** -0.5
    heads = []
    for h in range(M_HEADS):
        qh = (q_ref[:, h * M_DQK:(h + 1) * M_DQK].astype(F32) * scale).astype(BF16)
        kh = k_ref[:, h * M_DQK:(h + 1) * M_DQK]
        vh = v_ref[:, h * M_DV:(h + 1) * M_DV]
        b_row = cum_rows[h:h + 1, :]
        b_col = cum_cols[:, h:h + 1]
        i_row = rows[M_HEADS + h:M_HEADS + h + 1, :]
        m_prev = m_ref[h:h + 1, 0:1]
        c_prev = c_ref[h]
        n_prev = n_ref[h]

        d = jnp.where(tri, b_col - b_row + i_row, NEG_INF)
        inter = b_col + m_prev
        m_t = jnp.maximum(inter, jnp.max(d, axis=-1, keepdims=True))
        w_inter = jnp.exp(inter - m_t)
        s_mat = _nt_dot(qh, kh) * jnp.exp(d - m_t)
        num = w_inter * jnp.dot(qh, c_prev.astype(BF16), preferred_element_type=F32) \
            + jnp.dot(s_mat.astype(BF16), vh, preferred_element_type=F32)
        qn = _nt_dot(qh, n_prev.astype(BF16))[:, 0:1]
        den = w_inter * qn + jnp.sum(s_mat, axis=-1, keepdims=True)
        hh = num / jnp.maximum(jnp.abs(den), jnp.exp(-m_t))

        b_last = b_row[:, L - 1:L]
        g_row = b_last - b_row + i_row
        m_new = jnp.maximum(b_last + m_prev, jnp.max(g_row, axis=-1, keepdims=True))
        decay = jnp.exp(b_last + m_prev - m_new)
        w_row = jnp.exp(g_row - m_new)
        wk_t = (kh.astype(F32).T * w_row).astype(BF16)
        c_ref[h] = decay * c_prev + jnp.dot(wk_t, vh, preferred_element_type=F32)
        n_ref[h] = decay * n_prev + jnp.dot(jnp.broadcast_to(w_row, (8, L)).astype(BF16), kh,
                                            preferred_element_type=F32)
        m_ref[h:h + 1, :] = jnp.broadcast_to(m_new, (1, 128))

        hn = _rms_rows(hh, hg_ref[:, h * M_DV:(h + 1) * M_DV])
        heads.append((hn * jax.nn.sigmoid(o_ref[:, h * M_DV:(h + 1) * M_DV].astype(F32))).astype(BF16))

    hcat = jnp.concatenate(heads, axis=1)
    y_ref[...] = x_ref[...] + jnp.dot(hcat, wout_ref[...], preferred_element_type=F32)


def _mlstm_core(x, u, gates_t, bias, h_gain, w_out):
    s = x.shape[0]
    L = M_CHUNK
    return pl.pallas_call(
        _mlstm_core_kernel,
        grid=(s // L,),
        in_specs=[
            pl.BlockSpec((L, D_MODEL), lambda c: (c, 0)),
            pl.BlockSpec((L, M_HEADS * M_DQK), lambda c: (c, 0)),
            pl.BlockSpec((L, M_HEADS * M_DQK), lambda c: (c, 1)),
            pl.BlockSpec((L, M_HEADS * M_DV), lambda c: (c, 1)),
            pl.BlockSpec((L, M_HEADS * M_DV), lambda c: (c, 2)),
            pl.BlockSpec((8, L), lambda c: (0, c)),
            pl.BlockSpec((8, 1), lambda c: (0, 0)),
            pl.BlockSpec((1, D_MODEL), lambda c: (0, 0)),
            pl.BlockSpec((D_MODEL, D_MODEL), lambda c: (0, 0)),
        ],
        out_specs=pl.BlockSpec((L, D_MODEL), lambda c: (c, 0)),
        out_shape=jax.ShapeDtypeStruct((s, D_MODEL), F32),
        scratch_shapes=[
            pltpu.VMEM((M_HEADS, M_DQK, M_DV), F32),
            pltpu.VMEM((M_HEADS, 8, M_DQK), F32),
            pltpu.VMEM((8, 128), F32),
        ],
        compiler_params=pltpu.CompilerParams(
            dimension_semantics=("arbitrary",), vmem_limit_bytes=VMEM_LIMIT),
        name="mlstm_core",
    )(x, u, u, u, u, gates_t, bias, h_gain, w_out)


def _conv_ffn_kernel(x_ref, g_ref, wup_ref, cw_ref, cb_ref, wdown_ref, y_ref, carry_ref, act_ref):
    tm = x_ref.shape[0]

    @pl.when(pl.program_id(0) == 0)
    def _():
        carry_ref[...] = jnp.zeros_like(carry_ref)

    x = x_ref[...]
    xn = _rms_rows(x, g_ref[...]).astype(BF16)
    top = lax.broadcasted_iota(jnp.int32, (8, FFN_CHUNK), 0)

    def conv_cols(col):
        u = jnp.dot(xn, wup_ref[:, col:col + FFN_CHUNK], preferred_element_type=F32)
        prev = carry_ref[:, col:col + FFN_CHUNK]
        u1 = pltpu.roll(u, 1, 0)
        u2 = pltpu.roll(u, 2, 0)
        u1_top = jnp.where(top == 0, prev[7:8], u1[0:8])
        u2_top = jnp.where(top == 0, prev[6:7], jnp.where(top == 1, prev[7:8], u2[0:8]))
        u1 = jnp.concatenate([u1_top, u1[8:]], axis=0)
        u2 = jnp.concatenate([u2_top, u2[8:]], axis=0)
        carry_ref[:, col:col + FFN_CHUNK] = u[tm - 8:tm]
        cw = cw_ref[:, col:col + FFN_CHUNK]
        return cb_ref[:, col:col + FFN_CHUNK] + u2 * cw[0:1] + u1 * cw[1:2] + u * cw[2:3]

    for c in range(D_FF // FFN_CHUNK):
        val = conv_cols(c * FFN_CHUNK)
        gate = conv_cols(D_FF + c * FFN_CHUNK)
        act_ref[:, c * FFN_CHUNK:(c + 1) * FFN_CHUNK] = (gate * jax.nn.sigmoid(gate) * val).astype(BF16)

    y_ref[...] = x + jnp.dot(act_ref[...], wdown_ref[...], preferred_element_type=F32)


def _conv_ffn(x, gain, w_up, conv_w, conv_b, w_down):
    s = x.shape[0]
    const = lambda i: (0, 0)
    return pl.pallas_call(
        _conv_ffn_kernel,
        grid=(s // ROW_TILE,),
        in_specs=[
            pl.BlockSpec((ROW_TILE, D_MODEL), lambda i: (i, 0)),
            pl.BlockSpec((1, D_MODEL), const),
            pl.BlockSpec((D_MODEL, 2 * D_FF), const, pipeline_mode=pl.Buffered(1)),
            pl.BlockSpec((CONV_W, 2 * D_FF), const),
            pl.BlockSpec((1, 2 * D_FF), const),
            pl.BlockSpec((D_FF, D_MODEL), const, pipeline_mode=pl.Buffered(1)),
        ],
        out_specs=pl.BlockSpec((ROW_TILE, D_MODEL), lambda i: (i, 0)),
        out_shape=jax.ShapeDtypeStruct((s, D_MODEL), F32),
        scratch_shapes=[pltpu.VMEM((8, 2 * D_FF), F32), pltpu.VMEM((ROW_TILE, D_FF), BF16)],
        compiler_params=pltpu.CompilerParams(
            dimension_semantics=("arbitrary",), vmem_limit_bytes=VMEM_LIMIT),
        name="conv_ffn",
    )(x, gain, w_up, conv_w, conv_b, w_down)


def _kv_proj_kernel(x_ref, g_ref, w_ref, kg_ref, rc_ref, rs1_ref, rs2_ref, k_ref, vt_ref, km_ref, ksq_ref):
    nb = k_ref.shape[0]
    xn = _rms_rows(x_ref[...], g_ref[...]).astype(BF16)
    t = jnp.dot(xn, w_ref[:, :D_MODEL], preferred_element_type=F32)
    k = _headnorm_rope(t, kg_ref[...], rc_ref[...], rs1_ref[...], rs2_ref[...], 1.0)
    kb = k.astype(BF16)
    k = k.reshape(nb, MOBA_BLOCK, D_MODEL)
    km_ref[...] = jnp.mean(k, axis=1, keepdims=True)
    k_ref[...] = kb.reshape(nb, MOBA_BLOCK, D_MODEL)
    kf = kb.astype(F32)
    sq = []
    for h in range(A_HEADS):
        kh = kf[:, h * A_DH:(h + 1) * A_DH]
        n2 = jnp.sum(kh * kh, axis=-1, keepdims=True).reshape(nb, MOBA_BLOCK, 1)
        sq.append(jnp.broadcast_to(jnp.max(n2, axis=1, keepdims=True), (nb, 1, A_DH)))
    ksq_ref[...] = jnp.concatenate(sq, axis=2)

    vt = jnp.dot(xn, w_ref[:, D_MODEL:], preferred_element_type=F32).T
    for b in range(nb):
        vt_ref[b] = vt[:, b * MOBA_BLOCK:(b + 1) * MOBA_BLOCK].astype(BF16)


def _kv_proj(x, gain, w_kv, k_gain, rc, rs1, rs2):
    s = x.shape[0]
    nb = ROW_TILE // MOBA_BLOCK
    n_blocks = s // MOBA_BLOCK
    return pl.pallas_call(
        _kv_proj_kernel,
        grid=(s // ROW_TILE,),
        in_specs=[
            pl.BlockSpec((ROW_TILE, D_MODEL), lambda i: (i, 0)),
            pl.BlockSpec((1, D_MODEL), lambda i: (0, 0)),
            pl.BlockSpec((D_MODEL, 2 * D_MODEL), lambda i: (0, 0), pipeline_mode=pl.Buffered(1)),
            pl.BlockSpec((1, A_DH), lambda i: (0, 0)),
            pl.BlockSpec((ROW_TILE, A_DH), lambda i: (i, 0)),
            pl.BlockSpec((ROW_TILE, A_DH), lambda i: (i, 0)),
            pl.BlockSpec((ROW_TILE, A_DH), lambda i: (i, 0)),
        ],
        out_specs=[
            pl.BlockSpec((nb, MOBA_BLOCK, D_MODEL), lambda i: (i, 0, 0)),
            pl.BlockSpec((nb, D_MODEL, MOBA_BLOCK), lambda i: (i, 0, 0)),
            pl.BlockSpec((nb, 1, D_MODEL), lambda i: (i, 0, 0)),
            pl.BlockSpec((nb, 1, D_MODEL), lambda i: (i, 0, 0)),
        ],
        out_shape=[
            jax.ShapeDtypeStruct((n_blocks, MOBA_BLOCK, D_MODEL), BF16),
            jax.ShapeDtypeStruct((n_blocks, D_MODEL, MOBA_BLOCK), BF16),
            jax.ShapeDtypeStruct((n_blocks, 1, D_MODEL), F32),
            jax.ShapeDtypeStruct((n_blocks, 1, D_MODEL), F32),
        ],
        compiler_params=pltpu.CompilerParams(
            dimension_semantics=("parallel",), vmem_limit_bytes=VMEM_LIMIT),
        name="kv_proj",
    )(x, gain, w_kv, k_gain, rc, rs1, rs2)


def _q_proj_kernel(x_ref, g_ref, w_ref, qg_ref, rc_ref, rs1_ref, rs2_ref, qt_ref):
    xn = _rms_rows(x_ref[...], g_ref[...]).astype(BF16)
    t = jnp.dot(xn, w_ref[...], preferred_element_type=F32)
    q = _headnorm_rope(t, qg_ref[...], rc_ref[...], rs1_ref[...], rs2_ref[...], A_DH ** -0.5 * LOG2_E)
    qt_ref[...] = q.T.astype(BF16)


def _q_proj(x, gain, w_q, q_gain, rc, rs1, rs2):
    s = x.shape[0]
    return pl.pallas_call(
        _q_proj_kernel,
        grid=(s // ROW_TILE,),
        in_specs=[
            pl.BlockSpec((ROW_TILE, D_MODEL), lambda i: (i, 0)),
            pl.BlockSpec((1, D_MODEL), lambda i: (0, 0)),
            pl.BlockSpec((D_MODEL, D_MODEL), lambda i: (0, 0)),
            pl.BlockSpec((1, A_DH), lambda i: (0, 0)),
            pl.BlockSpec((ROW_TILE, A_DH), lambda i: (i, 0)),
            pl.BlockSpec((ROW_TILE, A_DH), lambda i: (i, 0)),
            pl.BlockSpec((ROW_TILE, A_DH), lambda i: (i, 0)),
        ],
        out_specs=pl.BlockSpec((D_MODEL, ROW_TILE), lambda i: (0, i)),
        out_shape=jax.ShapeDtypeStruct((D_MODEL, s), BF16),
        compiler_params=pltpu.CompilerParams(
            dimension_semantics=("parallel",), vmem_limit_bytes=VMEM_LIMIT),
        name="q_proj",
    )(x, gain, w_q, q_gain, rc, rs1, rs2)


def _moba_attn_kernel(qt_ref, k_ref, vt_ref, km_ref, ksq_ref, o_ref, sel_ref, bias_ref):
    nb = k_ref.shape[0]
    bs = MOBA_BLOCK
    cur = pl.program_id(1)
    kpos = lax.broadcasted_iota(jnp.int32, (bs, bs), 0)
    qpos = lax.broadcasted_iota(jnp.int32, (bs, bs), 1)
    causal = kpos <= qpos

    qts, own, gaps = [], [], []
    for hh in range(ATTN_HEADS):
        hs = slice(hh * A_DH, (hh + 1) * A_DH)
        qt = qt_ref[hs, :]
        qts.append(qt)
        gate = jnp.dot(km_ref[:, hs], qt, preferred_element_type=F32)
        blk = lax.broadcasted_iota(jnp.int32, gate.shape, 0)
        gate = jnp.where(blk < cur, gate, NEG_INF)
        sel = jnp.zeros(gate.shape, F32)
        for _ in range(MOBA_TOPK):
            top = jnp.max(gate, axis=0, keepdims=True)
            first = jnp.min(jnp.where((gate == top) & (gate > NEG_INF), blk, nb), axis=0, keepdims=True)
            pick = blk == first
            sel = jnp.where(pick, 1.0, sel)
            gate = jnp.where(pick, NEG_INF, gate)
        sel_ref[hh] = sel

        s_own = jnp.dot(k_ref[cur, :, hs], qt, preferred_element_type=F32)
        s_own = jnp.where(causal, s_own, NEG_INF)
        m0 = jnp.max(s_own, axis=0, keepdims=True)
        own.append((m0, jnp.exp2(s_own - m0)))
        bias_ref[hh] = jnp.where(sel > 0.0, -m0, MASKED_SCORE)

        qf = qt.astype(F32)
        q2 = jnp.sum(qf * qf, axis=0, keepdims=True)
        k2 = jnp.max(ksq_ref[:, hs], axis=0, keepdims=True)[:, 0:1]
        gaps.append(jnp.max(jnp.sqrt(q2 * k2) * NORM_BOUND_SLACK - m0))

    def finish(parts):
        o_ref[...] = jnp.concatenate([(acc / l).T for acc, l in parts], axis=1).astype(BF16)

    def fixed_max_path():
        ones = jnp.ones((ONES_ROWS, bs), BF16)
        init = []
        for hh in range(ATTN_HEADS):
            hs = slice(hh * A_DH, (hh + 1) * A_DH)
            vt_own = jnp.concatenate([vt_ref[cur, hs, :], ones], axis=0)
            init.append(jnp.dot(vt_own, own[hh][1].astype(BF16), preferred_element_type=F32))

        def body(step, accs):
            j0 = step * FAST_UNROLL
            out = []
            for hh in range(ATTN_HEADS):
                hs = slice(hh * A_DH, (hh + 1) * A_DH)
                p_parts, vt_parts = [], []
                for b in range(FAST_UNROLL):
                    s_b = jnp.dot(k_ref[j0 + b, :, hs], qts[hh], preferred_element_type=F32)
                    p_parts.append(jnp.exp2(s_b + bias_ref[hh, pl.ds(j0 + b, 1), :]).astype(BF16))
                    vt_parts.append(jnp.concatenate([vt_ref[j0 + b, hs, :], ones], axis=0))
                out.append(accs[hh] + jnp.dot(jnp.concatenate(vt_parts, axis=1), jnp.concatenate(p_parts, axis=0),
                                              preferred_element_type=F32))
            return tuple(out)

        accs = lax.fori_loop(0, (cur + FAST_UNROLL - 1) // FAST_UNROLL, body, tuple(init))
        finish([(acc[:A_DH], acc[A_DH:A_DH + 1]) for acc in accs])

    def running_max_path():
        init = []
        for hh in range(ATTN_HEADS):
            hs = slice(hh * A_DH, (hh + 1) * A_DH)
            m0, p0 = own[hh]
            init.append((m0, jnp.sum(p0, axis=0, keepdims=True),
                         jnp.dot(vt_ref[cur, hs, :], p0.astype(BF16), preferred_element_type=F32)))

        def body(step, carry):
            j0 = step * ATTN_UNROLL
            out = []
            for hh in range(ATTN_HEADS):
                hs = slice(hh * A_DH, (hh + 1) * A_DH)
                m, l, acc = carry[hh]
                s_parts, vt_parts = [], []
                for b in range(ATTN_UNROLL):
                    s_b = jnp.dot(k_ref[j0 + b, :, hs], qts[hh], preferred_element_type=F32)
                    s_parts.append(jnp.where(sel_ref[hh, pl.ds(j0 + b, 1), :] > 0.0, s_b, NEG_INF))
                    vt_parts.append(vt_ref[j0 + b, hs, :])
                s_t = jnp.concatenate(s_parts, axis=0)
                m_new = jnp.maximum(m, jnp.max(s_t, axis=0, keepdims=True))
                alpha = jnp.exp2(m - m_new)
                p = jnp.exp2(s_t - m_new)
                l = alpha * l + jnp.sum(p, axis=0, keepdims=True)
                acc = alpha * acc + jnp.dot(jnp.concatenate(vt_parts, axis=1), p.astype(BF16),
                                            preferred_element_type=F32)
                out.append((m_new, l, acc))
            return tuple(out)

        final = lax.fori_loop(0, (cur + ATTN_UNROLL - 1) // ATTN_UNROLL, body, tuple(init))
        finish([(acc, l) for _, l, acc in final])

    worst_gap = functools.reduce(jnp.maximum, gaps)
    lax.cond(worst_gap <= MAX_EXP2_ARG, fixed_max_path, running_max_path)


def _moba_attn(qt, kb, vtb, kmean, ksq):
    s = qt.shape[1]
    nb = s // MOBA_BLOCK
    gw = ATTN_HEADS * A_DH
    return pl.pallas_call(
        _moba_attn_kernel,
        grid=(A_HEADS // ATTN_HEADS, nb),
        in_specs=[
            pl.BlockSpec((gw, MOBA_BLOCK), lambda g, i: (g, i)),
            pl.BlockSpec((nb, MOBA_BLOCK, gw), lambda g, i: (0, 0, g), pipeline_mode=pl.Buffered(1)),
            pl.BlockSpec((nb, gw, MOBA_BLOCK), lambda g, i: (0, g, 0), pipeline_mode=pl.Buffered(1)),
            pl.BlockSpec((nb, gw), lambda g, i: (0, g)),
            pl.BlockSpec((nb, gw), lambda g, i: (0, g)),
        ],
        out_specs=pl.BlockSpec((MOBA_BLOCK, gw), lambda g, i: (i, g)),
        out_shape=jax.ShapeDtypeStruct((s, D_MODEL), BF16),
        scratch_shapes=[pltpu.VMEM((ATTN_HEADS, nb, MOBA_BLOCK), F32),
                        pltpu.VMEM((ATTN_HEADS, nb, MOBA_BLOCK), F32)],
        compiler_params=pltpu.CompilerParams(
            dimension_semantics=("parallel", "arbitrary"), vmem_limit_bytes=VMEM_LIMIT),
        name="moba_attn",
    )(qt, kb, vtb, kmean, ksq)


def _proj_residual_kernel(x_ref, a_ref, w_ref, y_ref):
    y_ref[...] = x_ref[...] + jnp.dot(a_ref[...], w_ref[...], preferred_element_type=F32)


def _proj_residual(x, a, w):
    s = x.shape[0]
    return pl.pallas_call(
        _proj_residual_kernel,
        grid=(s // ROW_TILE,),
        in_specs=[
            pl.BlockSpec((ROW_TILE, D_MODEL), lambda i: (i, 0)),
            pl.BlockSpec((ROW_TILE, D_MODEL), lambda i: (i, 0)),
            pl.BlockSpec((D_MODEL, D_MODEL), lambda i: (0, 0)),
        ],
        out_specs=pl.BlockSpec((ROW_TILE, D_MODEL), lambda i: (i, 0)),
        out_shape=jax.ShapeDtypeStruct((s, D_MODEL), F32),
        compiler_params=pltpu.CompilerParams(
            dimension_semantics=("parallel",), vmem_limit_bytes=VMEM_LIMIT),
        name="proj_residual",
    )(x, a, w)


def _rope_lane_tables(s_len):
    pos = jnp.arange(s_len, dtype=F32)
    inv = ROPE_THETA ** (-jnp.arange(0, ROT_DIM, 2, dtype=F32) / ROT_DIM)
    ang = pos[:, None] * inv[None, :]
    cos, sin = jnp.cos(ang), jnp.sin(ang)
    half = ROT_DIM // 2
    rc = jnp.concatenate([cos, cos, jnp.ones((s_len, A_DH - ROT_DIM), F32)], axis=1)
    rs1 = jnp.concatenate([-sin, jnp.zeros((s_len, A_DH - half), F32)], axis=1)
    rs2 = jnp.concatenate([jnp.zeros((s_len, half), F32), sin, jnp.zeros((s_len, A_DH - ROT_DIM), F32)], axis=1)
    return rc, rs1, rs2


def kernel(x, a_norm, a_w_in, a_b_gates, a_h_norm, a_w_out, kv_norm, w_kv, k_norm, b_norm, b_w_q, b_q_norm,
           b_w_o, f_norm, f_w_up, f_conv_w, f_conv_b, f_w_down):
    b_, s_, d_ = x.shape
    assert (b_, d_) == (1, D_MODEL) and s_ % ROW_TILE == 0 and s_ % MOBA_BLOCK == 0
    rc, rs1, rs2 = _rope_lane_tables(s_)
    h = x.reshape(s_, d_)
    kb = vtb = kmean = ksq = None
    for l in range(DEPTH):
        if l < N_A:
            w_in = a_w_in[l]
            wg = w_in[:, M_QKVO:]
            wg_hi = wg.astype(BF16)
            wg_lo = (wg - wg_hi.astype(F32)).astype(BF16)
            wg_cols = jnp.concatenate([wg_hi, wg_lo, jnp.zeros((d_, 128 - 4 * M_HEADS), BF16)], axis=1)
            u, gates_t = _mlstm_in_proj(h, a_norm[l][None], w_in[:, :M_QKVO].astype(BF16), wg_cols)
            h = _mlstm_core(h, u, gates_t, a_b_gates[l].reshape(2 * M_HEADS, 1), a_h_norm[l][None],
                            a_w_out[l].astype(BF16))
        else:
            j = l - N_A
            qt = _q_proj(h, b_norm[j][None], b_w_q[j].astype(BF16), b_q_norm[j][None], rc, rs1, rs2)
            o = _moba_attn(qt, kb, vtb, kmean, ksq)
            h = _proj_residual(h, o, b_w_o[j].astype(BF16))
        h = _conv_ffn(h, f_norm[l][None], f_w_up[l].astype(BF16), f_conv_w[l], f_conv_b[l][None],
                      f_w_down[l].astype(BF16))
        if l == N_A - 1:
            kb, vtb, km, ks = _kv_proj(h, kv_norm[None], w_kv.astype(BF16), k_norm[None], rc, rs1, rs2)
            kmean = km.reshape(s_ // MOBA_BLOCK, D_MODEL).astype(BF16)
            ksq = ks.reshape(s_ // MOBA_BLOCK, D_MODEL)
    return h.reshape(b_, s_, d_)
```

```python
import functools

import jax
import jax.numpy as jnp
from jax import lax
from jax.experimental import pallas as pl
from jax.experimental.pallas import tpu as pltpu

F32 = jnp.float32
BF16 = jnp.bfloat16

D_MODEL = 1024
DEPTH = 4
N_A = DEPTH // 2
M_HEADS = 4
M_DQK = 128
M_DV = D_MODEL // M_HEADS
M_QKVO = 2 * M_HEADS * M_DQK + 2 * M_HEADS * M_DV
A_HEADS = 8
A_DH = D_MODEL // A_HEADS
ROT_DIM = A_DH // 4
ROPE_THETA = 500000.0
MOBA_BLOCK = 256
MOBA_TOPK = 3
D_FF = 2816
CONV_W = 3
EPS = 1e-6

M_CHUNK = 256
ROW_TILE = 512
FFN_CHUNK = 256
ATTN_HEADS = 4
ATTN_UNROLL = 2
FAST_UNROLL = 8
TAIL_UNROLL = 4
ONES_ROWS = 16
LOG2_E = 1.4426950408889634
MASKED_SCORE = -1e30
MAX_EXP2_ARG = 100.0
NORM_BOUND_SLACK = 1.0 + 2.0 ** -6
VMEM_LIMIT = 56 * 1024 * 1024
NEG_INF = float("-inf")
HIGHEST = lax.Precision.HIGHEST


def _rms_rows(x, gain):
    return x * lax.rsqrt(jnp.mean(x * x, axis=-1, keepdims=True) + EPS) * gain


def _nt_dot(a, b, **kw):
    return lax.dot_general(a, b, (((1,), (1,)), ((), ())), preferred_element_type=F32, **kw)


def _headnorm_rope_t(t, gain, cos, sin, scale):
    half = ROT_DIM // 2
    g = jnp.concatenate([gain] * (t.shape[1] // gain.shape[1]), axis=1)
    outs = []
    for h in range(t.shape[0] // A_DH):
        th = t[h * A_DH:(h + 1) * A_DH, :]
        th = th * lax.rsqrt(jnp.mean(th * th, axis=0, keepdims=True) + EPS) * g
        t1, t2 = th[:half], th[half:ROT_DIM]
        th = jnp.concatenate([t1 * cos - t2 * sin, t2 * cos + t1 * sin, th[ROT_DIM:]], axis=0)
        outs.append(th * scale if scale != 1.0 else th)
    return jnp.concatenate(outs, axis=0)


def _mlstm_in_kernel(x_ref, g_ref, w_ref, wg_ref, u_ref, gt_ref):
    xn = _rms_rows(x_ref[...], g_ref[...])
    xh = xn.astype(BF16)
    u_ref[...] = jnp.dot(xh, w_ref[...], preferred_element_type=F32).astype(BF16)
    xl = (xn - xh.astype(F32)).astype(BF16)
    wg = wg_ref[...]
    e = jnp.dot(xh, wg, preferred_element_type=F32) + jnp.dot(xl, wg, preferred_element_type=F32)
    et = e.T
    gt_ref[...] = et[0:2 * M_HEADS] + et[2 * M_HEADS:4 * M_HEADS]


def _mlstm_in_proj(x, gain, w_qkvo, w_gates):
    s = x.shape[0]
    return pl.pallas_call(
        _mlstm_in_kernel,
        grid=(s // ROW_TILE,),
        in_specs=[
            pl.BlockSpec((ROW_TILE, D_MODEL), lambda i: (i, 0)),
            pl.BlockSpec((1, D_MODEL), lambda i: (0, 0)),
            pl.BlockSpec((D_MODEL, M_QKVO), lambda i: (0, 0), pipeline_mode=pl.Buffered(1)),
            pl.BlockSpec((D_MODEL, 128), lambda i: (0, 0)),
        ],
        out_specs=[
            pl.BlockSpec((ROW_TILE, M_QKVO), lambda i: (i, 0)),
            pl.BlockSpec((2 * M_HEADS, ROW_TILE), lambda i: (0, i)),
        ],
        out_shape=[jax.ShapeDtypeStruct((s, M_QKVO), BF16), jax.ShapeDtypeStruct((2 * M_HEADS, s), F32)],
        compiler_params=pltpu.CompilerParams(
            dimension_semantics=("parallel",), vmem_limit_bytes=VMEM_LIMIT),
        name="mlstm_in_proj",
    )(x, gain, w_qkvo, w_gates)


def _mlstm_core_kernel(x_ref, q_ref, k_ref, v_ref, o_ref, gt_ref, bias_ref, hg_ref, wout_ref,
                       y_ref, c_ref, n_ref, m_ref):
    L = M_CHUNK

    @pl.when(pl.program_id(0) == 0)
    def _():
        c_ref[...] = jnp.zeros_like(c_ref)
        n_ref[...] = jnp.zeros_like(n_ref)
        m_ref[...] = jnp.zeros_like(m_ref)

    g = gt_ref[...] + bias_ref[...]
    fpre = g[M_HEADS:]
    lf = jnp.minimum(fpre, 0.0) - jnp.log1p(jnp.exp(-jnp.abs(fpre)))
    rows = jnp.concatenate([lf, g[:M_HEADS]], axis=0)
    r_i = lax.broadcasted_iota(jnp.int32, (L, L), 0)
    c_i = lax.broadcasted_iota(jnp.int32, (L, L), 1)
    tri = c_i <= r_i
    tri_l = tri.astype(F32)
    tri_u = (r_i <= c_i).astype(F32)
    cum_rows = jnp.dot(rows, tri_u, preferred_element_type=F32, precision=HIGHEST)
    cum_cols = _nt_dot(tri_l, rows, precision=HIGHEST)

    scale = M_DQK ** -0.5
    heads = []
    for h in range(M_HEADS):
        qh = (q_ref[:, h * M_DQK:(h + 1) * M_DQK].astype(F32) * scale).astype(BF16)
        kh = k_ref[:, h * M_DQK:(h + 1) * M_DQK]
        vh = v_ref[:, h * M_DV:(h + 1) * M_DV]
        b_row = cum_rows[h:h + 1, :]
        b_col = cum_cols[:, h:h + 1]
        i_row = rows[M_HEADS + h:M_HEADS + h + 1, :]
        m_prev = m_ref[h:h + 1, 0:1]
        c_prev = c_ref[h]
        n_prev = n_ref[h]

        d = jnp.where(tri, b_col - b_row + i_row, NEG_INF)
        inter = b_col + m_prev
        m_t = jnp.maximum(inter, jnp.max(d, axis=-1, keepdims=True))
        w_inter = jnp.exp(inter - m_t)
        s_mat = _nt_dot(qh, kh) * jnp.exp(d - m_t)
        num = w_inter * jnp.dot(qh, c_prev.astype(BF16), preferred_element_type=F32) \
            + jnp.dot(s_mat.astype(BF16), vh, preferred_element_type=F32)
        qn = _nt_dot(qh, n_prev.astype(BF16))[:, 0:1]
        den = w_inter * qn + jnp.sum(s_mat, axis=-1, keepdims=True)
        hh = num / jnp.maximum(jnp.abs(den), jnp.exp(-m_t))

        b_last = b_row[:, L - 1:L]
        g_row = b_last - b_row + i_row
        m_new = jnp.maximum(b_last + m_prev, jnp.max(g_row, axis=-1, keepdims=True))
        decay = jnp.exp(b_last + m_prev - m_new)
        w_row = jnp.exp(g_row - m_new)
        wk_t = (kh.astype(F32).T * w_row).astype(BF16)
        c_ref[h] = decay * c_prev + jnp.dot(wk_t, vh, preferred_element_type=F32)
        n_ref[h] = decay * n_prev + jnp.dot(jnp.broadcast_to(w_row, (8, L)).astype(BF16), kh,
                                            preferred_element_type=F32)
        m_ref[h:h + 1, :] = jnp.broadcast_to(m_new, (1, 128))

        hn = _rms_rows(hh, hg_ref[:, h * M_DV:(h + 1) * M_DV])
        heads.append((hn * jax.nn.sigmoid(o_ref[:, h * M_DV:(h + 1) * M_DV].astype(F32))).astype(BF16))

    hcat = jnp.concatenate(heads, axis=1)
    y_ref[...] = x_ref[...] + jnp.dot(hcat, wout_ref[...], preferred_element_type=F32)


def _mlstm_core(x, u, gates_t, bias, h_gain, w_out):
    s = x.shape[0]
    L = M_CHUNK
    return pl.pallas_call(
        _mlstm_core_kernel,
        grid=(s // L,),
        in_specs=[
            pl.BlockSpec((L, D_MODEL), lambda c: (c, 0)),
            pl.BlockSpec((L, M_HEADS * M_DQK), lambda c: (c, 0)),
            pl.BlockSpec((L, M_HEADS * M_DQK), lambda c: (c, 1)),
            pl.BlockSpec((L, M_HEADS * M_DV), lambda c: (c, 1)),
            pl.BlockSpec((L, M_HEADS * M_DV), lambda c: (c, 2)),
            pl.BlockSpec((8, L), lambda c: (0, c)),
            pl.BlockSpec((8, 1), lambda c: (0, 0)),
            pl.BlockSpec((1, D_MODEL), lambda c: (0, 0)),
            pl.BlockSpec((D_MODEL, D_MODEL), lambda c: (0, 0)),
        ],
        out_specs=pl.BlockSpec((L, D_MODEL), lambda c: (c, 0)),
        out_shape=jax.ShapeDtypeStruct((s, D_MODEL), F32),
        scratch_shapes=[
            pltpu.VMEM((M_HEADS, M_DQK, M_DV), F32),
            pltpu.VMEM((M_HEADS, 8, M_DQK), F32),
            pltpu.VMEM((8, 128), F32),
        ],
        compiler_params=pltpu.CompilerParams(
            dimension_semantics=("arbitrary",), vmem_limit_bytes=VMEM_LIMIT),
        name="mlstm_core",
    )(x, u, u, u, u, gates_t, bias, h_gain, w_out)


def _conv_ffn_kernel(x_ref, g_ref, wup_ref, cw_ref, cb_ref, wdown_ref, y_ref, carry_ref, act_ref):
    tm = x_ref.shape[0]

    @pl.when(pl.program_id(0) == 0)
    def _():
        carry_ref[...] = jnp.zeros_like(carry_ref)

    x = x_ref[...]
    xn = _rms_rows(x, g_ref[...]).astype(BF16)
    top = lax.broadcasted_iota(jnp.int32, (8, FFN_CHUNK), 0)

    def conv_cols(col):
        u = jnp.dot(xn, wup_ref[:, col:col + FFN_CHUNK], preferred_element_type=F32)
        prev = carry_ref[:, col:col + FFN_CHUNK]
        u1 = pltpu.roll(u, 1, 0)
        u2 = pltpu.roll(u, 2, 0)
        u1_top = jnp.where(top == 0, prev[7:8], u1[0:8])
        u2_top = jnp.where(top == 0, prev[6:7], jnp.where(top == 1, prev[7:8], u2[0:8]))
        u1 = jnp.concatenate([u1_top, u1[8:]], axis=0)
        u2 = jnp.concatenate([u2_top, u2[8:]], axis=0)
        carry_ref[:, col:col + FFN_CHUNK] = u[tm - 8:tm]
        cw = cw_ref[:, col:col + FFN_CHUNK]
        return cb_ref[:, col:col + FFN_CHUNK] + u2 * cw[0:1] + u1 * cw[1:2] + u * cw[2:3]

    for c in range(D_FF // FFN_CHUNK):
        val = conv_cols(c * FFN_CHUNK)
        gate = conv_cols(D_FF + c * FFN_CHUNK)
        act_ref[:, c * FFN_CHUNK:(c + 1) * FFN_CHUNK] = (gate * jax.nn.sigmoid(gate) * val).astype(BF16)

    y_ref[...] = x + jnp.dot(act_ref[...], wdown_ref[...], preferred_element_type=F32)


def _conv_ffn(x, gain, w_up, conv_w, conv_b, w_down):
    s = x.shape[0]
    const = lambda i: (0, 0)
    return pl.pallas_call(
        _conv_ffn_kernel,
        grid=(s // ROW_TILE,),
        in_specs=[
            pl.BlockSpec((ROW_TILE, D_MODEL), lambda i: (i, 0)),
            pl.BlockSpec((1, D_MODEL), const),
            pl.BlockSpec((D_MODEL, 2 * D_FF), const, pipeline_mode=pl.Buffered(1)),
            pl.BlockSpec((CONV_W, 2 * D_FF), const),
            pl.BlockSpec((1, 2 * D_FF), const),
            pl.BlockSpec((D_FF, D_MODEL), const, pipeline_mode=pl.Buffered(1)),
        ],
        out_specs=pl.BlockSpec((ROW_TILE, D_MODEL), lambda i: (i, 0)),
        out_shape=jax.ShapeDtypeStruct((s, D_MODEL), F32),
        scratch_shapes=[pltpu.VMEM((8, 2 * D_FF), F32), pltpu.VMEM((ROW_TILE, D_FF), BF16)],
        compiler_params=pltpu.CompilerParams(
            dimension_semantics=("arbitrary",), vmem_limit_bytes=VMEM_LIMIT),
        name="conv_ffn",
    )(x, gain, w_up, conv_w, conv_b, w_down)


def _kv_proj_kernel(x_ref, g_ref, wt_ref, kg_ref, cos_ref, sin_ref, k_ref, vt_ref, km_ref, ksq_ref):
    nb = k_ref.shape[0]
    xn = _rms_rows(x_ref[...], g_ref[...]).astype(BF16)
    kt = _headnorm_rope_t(_nt_dot(wt_ref[:D_MODEL, :], xn), kg_ref[...], cos_ref[...], sin_ref[...], 1.0)
    kbt = kt.astype(BF16)
    k = kt.T.reshape(nb, MOBA_BLOCK, D_MODEL)
    km_ref[...] = jnp.mean(k, axis=1, keepdims=True)
    k_ref[...] = k.astype(BF16)
    kf = kbt.astype(F32)
    rows = []
    for b in range(nb):
        per_head = []
        for h in range(A_HEADS):
            kh = kf[h * A_DH:(h + 1) * A_DH, b * MOBA_BLOCK:(b + 1) * MOBA_BLOCK]
            n2 = jnp.max(jnp.sum(kh * kh, axis=0, keepdims=True), axis=1, keepdims=True)
            per_head.append(jnp.broadcast_to(n2, (1, A_DH)))
        rows.append(jnp.concatenate(per_head, axis=1)[None])
    ksq_ref[...] = jnp.concatenate(rows, axis=0)

    vt = _nt_dot(wt_ref[D_MODEL:, :], xn).astype(BF16)
    for b in range(nb):
        vt_ref[b] = vt[:, b * MOBA_BLOCK:(b + 1) * MOBA_BLOCK]


def _kv_proj(x, gain, wt_kv, k_gain, cos_t, sin_t):
    s = x.shape[0]
    nb = ROW_TILE // MOBA_BLOCK
    n_blocks = s // MOBA_BLOCK
    half = ROT_DIM // 2
    return pl.pallas_call(
        _kv_proj_kernel,
        grid=(s // ROW_TILE,),
        in_specs=[
            pl.BlockSpec((ROW_TILE, D_MODEL), lambda i: (i, 0)),
            pl.BlockSpec((1, D_MODEL), lambda i: (0, 0)),
            pl.BlockSpec((2 * D_MODEL, D_MODEL), lambda i: (0, 0), pipeline_mode=pl.Buffered(1)),
            pl.BlockSpec((A_DH, A_DH), lambda i: (0, 0)),
            pl.BlockSpec((half, ROW_TILE), lambda i: (0, i)),
            pl.BlockSpec((half, ROW_TILE), lambda i: (0, i)),
        ],
        out_specs=[
            pl.BlockSpec((nb, MOBA_BLOCK, D_MODEL), lambda i: (i, 0, 0)),
            pl.BlockSpec((nb, D_MODEL, MOBA_BLOCK), lambda i: (i, 0, 0)),
            pl.BlockSpec((nb, 1, D_MODEL), lambda i: (i, 0, 0)),
            pl.BlockSpec((nb, 1, D_MODEL), lambda i: (i, 0, 0)),
        ],
        out_shape=[
            jax.ShapeDtypeStruct((n_blocks, MOBA_BLOCK, D_MODEL), BF16),
            jax.ShapeDtypeStruct((n_blocks, D_MODEL, MOBA_BLOCK), BF16),
            jax.ShapeDtypeStruct((n_blocks, 1, D_MODEL), F32),
            jax.ShapeDtypeStruct((n_blocks, 1, D_MODEL), F32),
        ],
        compiler_params=pltpu.CompilerParams(
            dimension_semantics=("parallel",), vmem_limit_bytes=VMEM_LIMIT),
        name="kv_proj",
    )(x, gain, wt_kv, k_gain, cos_t, sin_t)


def _q_proj_kernel(x_ref, g_ref, wt_ref, qg_ref, cos_ref, sin_ref, qt_ref):
    xn = _rms_rows(x_ref[...], g_ref[...]).astype(BF16)
    qt = _headnorm_rope_t(_nt_dot(wt_ref[...], xn), qg_ref[...], cos_ref[...], sin_ref[...], A_DH ** -0.5 * LOG2_E)
    qt_ref[...] = qt.astype(BF16)


def _q_proj(x, gain, wt_q, q_gain, cos_t, sin_t):
    s = x.shape[0]
    half = ROT_DIM // 2
    return pl.pallas_call(
        _q_proj_kernel,
        grid=(s // ROW_TILE,),
        in_specs=[
            pl.BlockSpec((ROW_TILE, D_MODEL), lambda i: (i, 0)),
            pl.BlockSpec((1, D_MODEL), lambda i: (0, 0)),
            pl.BlockSpec((D_MODEL, D_MODEL), lambda i: (0, 0)),
            pl.BlockSpec((A_DH, A_DH), lambda i: (0, 0)),
            pl.BlockSpec((half, ROW_TILE), lambda i: (0, i)),
            pl.BlockSpec((half, ROW_TILE), lambda i: (0, i)),
        ],
        out_specs=pl.BlockSpec((D_MODEL, ROW_TILE), lambda i: (0, i)),
        out_shape=jax.ShapeDtypeStruct((D_MODEL, s), BF16),
        compiler_params=pltpu.CompilerParams(
            dimension_semantics=("parallel",), vmem_limit_bytes=VMEM_LIMIT),
        name="q_proj",
    )(x, gain, wt_q, q_gain, cos_t, sin_t)


def _moba_attn_kernel(qt_ref, k_ref, vt_ref, km_ref, ksq_ref, o_ref, sel_ref, bias_ref):
    nb = k_ref.shape[0]
    bs = MOBA_BLOCK
    cur = pl.program_id(1)
    kpos = lax.broadcasted_iota(jnp.int32, (bs, bs), 0)
    qpos = lax.broadcasted_iota(jnp.int32, (bs, bs), 1)
    causal = kpos <= qpos

    qts, own, gaps = [], [], []
    for hh in range(ATTN_HEADS):
        hs = slice(hh * A_DH, (hh + 1) * A_DH)
        qt = qt_ref[hs, :]
        qts.append(qt)
        gate = jnp.dot(km_ref[:, hs], qt, preferred_element_type=F32)
        blk = lax.broadcasted_iota(jnp.int32, gate.shape, 0)
        gate = jnp.where(blk < cur, gate, NEG_INF)
        sel = jnp.zeros(gate.shape, F32)
        for _ in range(MOBA_TOPK):
            top = jnp.max(gate, axis=0, keepdims=True)
            first = jnp.min(jnp.where((gate == top) & (gate > NEG_INF), blk, nb), axis=0, keepdims=True)
            pick = blk == first
            sel = jnp.where(pick, 1.0, sel)
            gate = jnp.where(pick, NEG_INF, gate)
        sel_ref[hh] = sel

        s_own = jnp.dot(k_ref[cur, :, hs], qt, preferred_element_type=F32)
        s_own = jnp.where(causal, s_own, NEG_INF)
        m0 = jnp.max(s_own, axis=0, keepdims=True)
        own.append((m0, jnp.exp2(s_own - m0)))
        bias_ref[hh] = jnp.where(sel > 0.0, -m0, MASKED_SCORE)

        qf = qt.astype(F32)
        q2 = jnp.sum(qf * qf, axis=0, keepdims=True)
        k2 = jnp.max(ksq_ref[:, hs], axis=0, keepdims=True)[:, 0:1]
        gaps.append(jnp.max(jnp.sqrt(q2 * k2) * NORM_BOUND_SLACK - m0))

    def finish(parts):
        o_ref[...] = jnp.concatenate([(acc / l).T for acc, l in parts], axis=1).astype(BF16)

    def fixed_max_path():
        ones = jnp.ones((ONES_ROWS, bs), BF16)
        init = []
        for hh in range(ATTN_HEADS):
            hs = slice(hh * A_DH, (hh + 1) * A_DH)
            vt_own = jnp.concatenate([vt_ref[cur, hs, :], ones], axis=0)
            init.append(jnp.dot(vt_own, own[hh][1].astype(BF16), preferred_element_type=F32))

        def make_body(unroll, base):
            def body(step, accs):
                j0 = base + step * unroll
                out = []
                for hh in range(ATTN_HEADS):
                    hs = slice(hh * A_DH, (hh + 1) * A_DH)
                    p_parts, vt_parts = [], []
                    for b in range(unroll):
                        s_b = jnp.dot(k_ref[j0 + b, :, hs], qts[hh], preferred_element_type=F32)
                        p_parts.append(jnp.exp2(s_b + bias_ref[hh, pl.ds(j0 + b, 1), :]).astype(BF16))
                        vt_parts.append(jnp.concatenate([vt_ref[j0 + b, hs, :], ones], axis=0))
                    out.append(accs[hh] + jnp.dot(jnp.concatenate(vt_parts, axis=1),
                                                  jnp.concatenate(p_parts, axis=0), preferred_element_type=F32))
                return tuple(out)
            return body

        n_main = cur // FAST_UNROLL
        accs = lax.fori_loop(0, n_main, make_body(FAST_UNROLL, 0), tuple(init))
        done = n_main * FAST_UNROLL
        accs = lax.fori_loop(0, (cur - done + TAIL_UNROLL - 1) // TAIL_UNROLL, make_body(TAIL_UNROLL, done), accs)
        finish([(acc[:A_DH], acc[A_DH:A_DH + 1]) for acc in accs])

    def running_max_path():
        init = []
        for hh in range(ATTN_HEADS):
            hs = slice(hh * A_DH, (hh + 1) * A_DH)
            m0, p0 = own[hh]
            init.append((m0, jnp.sum(p0, axis=0, keepdims=True),
                         jnp.dot(vt_ref[cur, hs, :], p0.astype(BF16), preferred_element_type=F32)))

        def body(step, carry):
            j0 = step * ATTN_UNROLL
            out = []
            for hh in range(ATTN_HEADS):
                hs = slice(hh * A_DH, (hh + 1) * A_DH)
                m, l, acc = carry[hh]
                s_parts, vt_parts = [], []
                for b in range(ATTN_UNROLL):
                    s_b = jnp.dot(k_ref[j0 + b, :, hs], qts[hh], preferred_element_type=F32)
                    s_parts.append(jnp.where(sel_ref[hh, pl.ds(j0 + b, 1), :] > 0.0, s_b, NEG_INF))
                    vt_parts.append(vt_ref[j0 + b, hs, :])
                s_t = jnp.concatenate(s_parts, axis=0)
                m_new = jnp.maximum(m, jnp.max(s_t, axis=0, keepdims=True))
                alpha = jnp.exp2(m - m_new)
                p = jnp.exp2(s_t - m_new)
                l = alpha * l + jnp.sum(p, axis=0, keepdims=True)
                acc = alpha * acc + jnp.dot(jnp.concatenate(vt_parts, axis=1), p.astype(BF16),
                                            preferred_element_type=F32)
                out.append((m_new, l, acc))
            return tuple(out)

        final = lax.fori_loop(0, (cur + ATTN_UNROLL - 1) // ATTN_UNROLL, body, tuple(init))
        finish([(acc, l) for _, l, acc in final])

    worst_gap = functools.reduce(jnp.maximum, gaps)
    lax.cond(worst_gap <= MAX_EXP2_ARG, fixed_max_path, running_max_path)


def _moba_attn(qt, kb, vtb, kmean, ksq):
    s = qt.shape[1]
    nb = s // MOBA_BLOCK
    gw = ATTN_HEADS * A_DH
    return pl.pallas_call(
        _moba_attn_kernel,
        grid=(A_HEADS // ATTN_HEADS, nb),
        in_specs=[
            pl.BlockSpec((gw, MOBA_BLOCK), lambda g, i: (g, i)),
            pl.BlockSpec((nb, MOBA_BLOCK, gw), lambda g, i: (0, 0, g), pipeline_mode=pl.Buffered(1)),
            pl.BlockSpec((nb, gw, MOBA_BLOCK), lambda g, i: (0, g, 0), pipeline_mode=pl.Buffered(1)),
            pl.BlockSpec((nb, gw), lambda g, i: (0, g)),
            pl.BlockSpec((nb, gw), lambda g, i: (0, g)),
        ],
        out_specs=pl.BlockSpec((MOBA_BLOCK, gw), lambda g, i: (i, g)),
        out_shape=jax.ShapeDtypeStruct((s, D_MODEL), BF16),
        scratch_shapes=[pltpu.VMEM((ATTN_HEADS, nb, MOBA_BLOCK), F32),
                        pltpu.VMEM((ATTN_HEADS, nb, MOBA_BLOCK), F32)],
        compiler_params=pltpu.CompilerParams(
            dimension_semantics=("parallel", "arbitrary"), vmem_limit_bytes=VMEM_LIMIT),
        name="moba_attn",
    )(qt, kb, vtb, kmean, ksq)


def _proj_residual_kernel(x_ref, a_ref, w_ref, y_ref):
    y_ref[...] = x_ref[...] + jnp.dot(a_ref[...], w_ref[...], preferred_element_type=F32)


def _proj_residual(x, a, w):
    s = x.shape[0]
    return pl.pallas_call(
        _proj_residual_kernel,
        grid=(s // ROW_TILE,),
        in_specs=[
            pl.BlockSpec((ROW_TILE, D_MODEL), lambda i: (i, 0)),
            pl.BlockSpec((ROW_TILE, D_MODEL), lambda i: (i, 0)),
            pl.BlockSpec((D_MODEL, D_MODEL), lambda i: (0, 0)),
        ],
        out_specs=pl.BlockSpec((ROW_TILE, D_MODEL), lambda i: (i, 0)),
        out_shape=jax.ShapeDtypeStruct((s, D_MODEL), F32),
        compiler_params=pltpu.CompilerParams(
            dimension_semantics=("parallel",), vmem_limit_bytes=VMEM_LIMIT),
        name="proj_residual",
    )(x, a, w)


def _rope_tables_t(s_len):
    pos = jnp.arange(s_len, dtype=F32)
    inv = ROPE_THETA ** (-jnp.arange(0, ROT_DIM, 2, dtype=F32) / ROT_DIM)
    ang = inv[:, None] * pos[None, :]
    return jnp.cos(ang), jnp.sin(ang)


def _lane_replicated(gain):
    return jnp.broadcast_to(gain[:, None], (gain.shape[0], 128))


def kernel(x, a_norm, a_w_in, a_b_gates, a_h_norm, a_w_out, kv_norm, w_kv, k_norm, b_norm, b_w_q, b_q_norm,
           b_w_o, f_norm, f_w_up, f_conv_w, f_conv_b, f_w_down):
    b_, s_, d_ = x.shape
    assert (b_, d_) == (1, D_MODEL) and s_ % ROW_TILE == 0 and s_ % MOBA_BLOCK == 0
    cos_t, sin_t = _rope_tables_t(s_)
    h = x.reshape(s_, d_)
    kb = vtb = kmean = ksq = None
    for l in range(DEPTH):
        if l < N_A:
            w_in = a_w_in[l]
            wg = w_in[:, M_QKVO:]
            wg_hi = wg.astype(BF16)
            wg_lo = (wg - wg_hi.astype(F32)).astype(BF16)
            wg_cols = jnp.concatenate([wg_hi, wg_lo, jnp.zeros((d_, 128 - 4 * M_HEADS), BF16)], axis=1)
            u, gates_t = _mlstm_in_proj(h, a_norm[l][None], w_in[:, :M_QKVO].astype(BF16), wg_cols)
            h = _mlstm_core(h, u, gates_t, a_b_gates[l].reshape(2 * M_HEADS, 1), a_h_norm[l][None],
                            a_w_out[l].astype(BF16))
        else:
            j = l - N_A
            qt = _q_proj(h, b_norm[j][None], b_w_q[j].T.astype(BF16), _lane_replicated(b_q_norm[j]), cos_t, sin_t)
            o = _moba_attn(qt, kb, vtb, kmean, ksq)
            h = _proj_residual(h, o, b_w_o[j].astype(BF16))
        h = _conv_ffn(h, f_norm[l][None], f_w_up[l].astype(BF16), f_conv_w[l], f_conv_b[l][None],
                      f_w_down[l].astype(BF16))
        if l == N_A - 1:
            kb, vtb, km, ks = _kv_proj(h, kv_norm[None], w_kv.T.astype(BF16), _lane_replicated(k_norm), cos_t, sin_t)
            kmean = km.reshape(s_ // MOBA_BLOCK, D_MODEL).astype(BF16)
            ksq = ks.reshape(s_ // MOBA_BLOCK, D_MODEL)
    return h.reshape(b_, s_, d_)
```

```python
import functools

import jax
import jax.numpy as jnp
from jax import lax
from jax.experimental import pallas as pl
from jax.experimental.pallas import tpu as pltpu

F32 = jnp.float32
BF16 = jnp.bfloat16

D_MODEL = 1024
DEPTH = 4
N_A = DEPTH // 2
M_HEADS = 4
M_DQK = 128
M_DV = D_MODEL // M_HEADS
M_QKVO = 2 * M_HEADS * M_DQK + 2 * M_HEADS * M_DV
M_UT_Q = 2 * M_HEADS * M_DV
M_UT_ROWS = M_UT_Q + M_HEADS * M_DQK
A_HEADS = 8
A_DH = D_MODEL // A_HEADS
ROT_DIM = A_DH // 4
ROPE_THETA = 500000.0
MOBA_BLOCK = 256
MOBA_TOPK = 3
D_FF = 2816
CONV_W = 3
EPS = 1e-6

M_CHUNK = 256
ROW_TILE = 512
FFN_ROW_TILE = 1024
FFN_CHUNK = 256
ATTN_HEADS = 4
ATTN_UNROLL = 2
FAST_UNROLL = 8
TAIL_UNROLL = 4
ONES_ROWS = 16
LOG2_E = 1.4426950408889634
MASKED_SCORE = -1e30
MAX_EXP2_ARG = 100.0
NORM_BOUND_SLACK = 1.0 + 2.0 ** -6
VMEM_LIMIT = 56 * 1024 * 1024
NEG_INF = float("-inf")
HIGHEST = lax.Precision.HIGHEST


def _rms_rows(x, gain):
    return x * lax.rsqrt(jnp.mean(x * x, axis=-1, keepdims=True) + EPS) * gain


def _nt_dot(a, b, **kw):
    return lax.dot_general(a, b, (((1,), (1,)), ((), ())), preferred_element_type=F32, **kw)


def _headnorm_rope_t(t, gain, cos, sin, scale):
    half = ROT_DIM // 2
    g = jnp.concatenate([gain] * (t.shape[1] // gain.shape[1]), axis=1)
    outs = []
    for h in range(t.shape[0] // A_DH):
        th = t[h * A_DH:(h + 1) * A_DH, :]
        th = th * lax.rsqrt(jnp.mean(th * th, axis=0, keepdims=True) + EPS) * g
        t1, t2 = th[:half], th[half:ROT_DIM]
        th = jnp.concatenate([t1 * cos - t2 * sin, t2 * cos + t1 * sin, th[ROT_DIM:]], axis=0)
        outs.append(th * scale if scale != 1.0 else th)
    return jnp.concatenate(outs, axis=0)


def _mlstm_in_kernel(x_ref, g_ref, wk_ref, wt_ref, wg_ref, k_ref, ut_ref, gt_ref):
    xn = _rms_rows(x_ref[...], g_ref[...])
    xh = xn.astype(BF16)
    k_ref[...] = jnp.dot(xh, wk_ref[...], preferred_element_type=F32).astype(BF16)
    ut = _nt_dot(wt_ref[...], xh)
    ut_ref[:M_UT_Q, :] = ut[:M_UT_Q].astype(BF16)
    ut_ref[M_UT_Q:, :] = (ut[M_UT_Q:] * M_DQK ** -0.5).astype(BF16)
    xl = (xn - xh.astype(F32)).astype(BF16)
    wg = wg_ref[...]
    e = jnp.dot(xh, wg, preferred_element_type=F32) + jnp.dot(xl, wg, preferred_element_type=F32)
    et = e.T
    gt_ref[...] = et[0:2 * M_HEADS] + et[2 * M_HEADS:4 * M_HEADS]


def _mlstm_in_proj(x, gain, w_k, wt_voq, w_gates):
    s = x.shape[0]
    dk = M_HEADS * M_DQK
    return pl.pallas_call(
        _mlstm_in_kernel,
        grid=(s // ROW_TILE,),
        in_specs=[
            pl.BlockSpec((ROW_TILE, D_MODEL), lambda i: (i, 0)),
            pl.BlockSpec((1, D_MODEL), lambda i: (0, 0)),
            pl.BlockSpec((D_MODEL, dk), lambda i: (0, 0), pipeline_mode=pl.Buffered(1)),
            pl.BlockSpec((M_UT_ROWS, D_MODEL), lambda i: (0, 0), pipeline_mode=pl.Buffered(1)),
            pl.BlockSpec((D_MODEL, 128), lambda i: (0, 0)),
        ],
        out_specs=[
            pl.BlockSpec((ROW_TILE, dk), lambda i: (i, 0)),
            pl.BlockSpec((M_UT_ROWS, ROW_TILE), lambda i: (0, i)),
            pl.BlockSpec((2 * M_HEADS, ROW_TILE), lambda i: (0, i)),
        ],
        out_shape=[jax.ShapeDtypeStruct((s, dk), BF16), jax.ShapeDtypeStruct((M_UT_ROWS, s), BF16),
                   jax.ShapeDtypeStruct((2 * M_HEADS, s), F32)],
        compiler_params=pltpu.CompilerParams(
            dimension_semantics=("parallel",), vmem_limit_bytes=VMEM_LIMIT),
        name="mlstm_in_proj",
    )(x, gain, w_k, wt_voq, w_gates)


def _mlstm_core_kernel(x_ref, k_ref, vt_ref, ot_ref, qt_ref, gt_ref, bias_ref, hg_ref, woutt_ref,
                       y_ref, ct_ref, n_ref, m_ref):
    L = M_CHUNK

    @pl.when(pl.program_id(0) == 0)
    def _():
        ct_ref[...] = jnp.zeros_like(ct_ref)
        n_ref[...] = jnp.zeros_like(n_ref)
        m_ref[...] = jnp.zeros_like(m_ref)

    g = gt_ref[...] + bias_ref[...]
    fpre = g[M_HEADS:]
    lf = jnp.minimum(fpre, 0.0) - jnp.log1p(jnp.exp(-jnp.abs(fpre)))
    gi = g[:M_HEADS]
    s_i = lax.broadcasted_iota(jnp.int32, (L, L), 0)
    t_i = lax.broadcasted_iota(jnp.int32, (L, L), 1)
    causal = s_i <= t_i
    lf8 = jnp.concatenate([lf, jnp.zeros((M_HEADS, L), F32)], axis=0)
    b_rows = jnp.dot(lf8, causal.astype(F32), preferred_element_type=F32,
                     precision=HIGHEST)
    a_rows = gi - b_rows[:M_HEADS]
    a_cols = jnp.concatenate([a_rows, jnp.zeros((128 - M_HEADS, L), F32)], axis=0).T
    hgain = jnp.concatenate([hg_ref[...]] * (L // 128), axis=1)

    heads = []
    for h in range(M_HEADS):
        qt = qt_ref[h * M_DQK:(h + 1) * M_DQK, :]
        kh = k_ref[:, h * M_DQK:(h + 1) * M_DQK]
        vt = vt_ref[h * M_DV:(h + 1) * M_DV, :]
        b_row = b_rows[h:h + 1, :]
        i_row = gi[h:h + 1, :]
        m_prev = m_ref[h:h + 1, 0:1]
        ct_prev = ct_ref[h]
        n_prev = n_ref[h]

        d = jnp.where(causal, a_cols[:, h:h + 1] + b_row, NEG_INF)
        inter = b_row + m_prev
        m_t = jnp.maximum(inter, jnp.max(d, axis=0, keepdims=True))
        w_inter = jnp.exp(inter - m_t)
        s_mat = jnp.dot(kh, qt, preferred_element_type=F32) * jnp.exp(d - m_t)
        num = w_inter * jnp.dot(ct_prev.astype(BF16), qt, preferred_element_type=F32) \
            + jnp.dot(vt, s_mat.astype(BF16), preferred_element_type=F32)
        qn = jnp.dot(n_prev.astype(BF16), qt, preferred_element_type=F32)[0:1, :]
        den = w_inter * qn + jnp.sum(s_mat, axis=0, keepdims=True)
        hh = num / jnp.maximum(jnp.abs(den), jnp.exp(-m_t))

        b_last = b_row[:, L - 1:L]
        g_row = b_last - b_row + i_row
        m_new = jnp.maximum(b_last + m_prev, jnp.max(g_row, axis=-1, keepdims=True))
        decay = jnp.exp(b_last + m_prev - m_new)
        w_row = jnp.exp(g_row - m_new)
        ct_ref[h] = decay * ct_prev + jnp.dot((vt.astype(F32) * w_row).astype(BF16), kh,
                                              preferred_element_type=F32)
        n_ref[h] = decay * n_prev + jnp.dot(jnp.broadcast_to(w_row, (8, L)).astype(BF16), kh,
                                            preferred_element_type=F32)
        m_ref[h:h + 1, :] = jnp.broadcast_to(m_new, (1, 128))

        hs = slice(h * M_DV, (h + 1) * M_DV)
        hn = hh * lax.rsqrt(jnp.mean(hh * hh, axis=0, keepdims=True) + EPS) * hgain[hs]
        heads.append((hn * jax.nn.sigmoid(ot_ref[hs, :].astype(F32))).astype(BF16))

    yt = jnp.dot(woutt_ref[...], jnp.concatenate(heads, axis=0), preferred_element_type=F32)
    y_ref[...] = x_ref[...] + yt.T


def _mlstm_core(x, k, ut, gates_t, bias, h_gain, wt_out):
    s = x.shape[0]
    L = M_CHUNK
    dk, dv = M_HEADS * M_DQK, M_HEADS * M_DV
    return pl.pallas_call(
        _mlstm_core_kernel,
        grid=(s // L,),
        in_specs=[
            pl.BlockSpec((L, D_MODEL), lambda c: (c, 0)),
            pl.BlockSpec((L, dk), lambda c: (c, 0)),
            pl.BlockSpec((dv, L), lambda c: (0, c)),
            pl.BlockSpec((dv, L), lambda c: (1, c)),
            pl.BlockSpec((dk, L), lambda c: (M_UT_Q // dk, c)),
            pl.BlockSpec((2 * M_HEADS, L), lambda c: (0, c)),
            pl.BlockSpec((2 * M_HEADS, 1), lambda c: (0, 0)),
            pl.BlockSpec((dv, 128), lambda c: (0, 0)),
            pl.BlockSpec((D_MODEL, D_MODEL), lambda c: (0, 0)),
        ],
        out_specs=pl.BlockSpec((L, D_MODEL), lambda c: (c, 0)),
        out_shape=jax.ShapeDtypeStruct((s, D_MODEL), F32),
        scratch_shapes=[
            pltpu.VMEM((M_HEADS, M_DV, M_DQK), F32),
            pltpu.VMEM((M_HEADS, 8, M_DQK), F32),
            pltpu.VMEM((8, 128), F32),
        ],
        compiler_params=pltpu.CompilerParams(
            dimension_semantics=("arbitrary",), vmem_limit_bytes=VMEM_LIMIT),
        name="mlstm_core",
    )(x, k, ut, ut, ut, gates_t, bias, h_gain, wt_out)


def _conv_ffn_kernel(*refs, mixer_proj):
    if mixer_proj:
        x_ref, a_ref, wo_ref, g_ref, wup_ref, cw_ref, cb_ref, wdown_ref, y_ref, carry_ref, act_ref = refs
    else:
        x_ref, g_ref, wup_ref, cw_ref, cb_ref, wdown_ref, y_ref, carry_ref, act_ref = refs
    tm = x_ref.shape[0]

    @pl.when(pl.program_id(0) == 0)
    def _():
        carry_ref[...] = jnp.zeros_like(carry_ref)

    x = x_ref[...]
    if mixer_proj:
        x = x + jnp.dot(a_ref[...], wo_ref[...], preferred_element_type=F32)
    xn = _rms_rows(x, g_ref[...]).astype(BF16)
    top = lax.broadcasted_iota(jnp.int32, (8, FFN_CHUNK), 0)

    def conv_cols(col):
        u = jnp.dot(xn, wup_ref[:, col:col + FFN_CHUNK], preferred_element_type=F32)
        prev = carry_ref[:, col:col + FFN_CHUNK]
        u1 = pltpu.roll(u, 1, 0)
        u2 = pltpu.roll(u, 2, 0)
        u1_top = jnp.where(top == 0, prev[7:8], u1[0:8])
        u2_top = jnp.where(top == 0, prev[6:7], jnp.where(top == 1, prev[7:8], u2[0:8]))
        u1 = jnp.concatenate([u1_top, u1[8:]], axis=0)
        u2 = jnp.concatenate([u2_top, u2[8:]], axis=0)
        carry_ref[:, col:col + FFN_CHUNK] = u[tm - 8:tm]
        cw = cw_ref[:, col:col + FFN_CHUNK]
        return cb_ref[:, col:col + FFN_CHUNK] + u2 * cw[0:1] + u1 * cw[1:2] + u * cw[2:3]

    for c in range(D_FF // FFN_CHUNK):
        val = conv_cols(c * FFN_CHUNK)
        gate = conv_cols(D_FF + c * FFN_CHUNK)
        act_ref[:, c * FFN_CHUNK:(c + 1) * FFN_CHUNK] = (gate * jax.nn.sigmoid(gate) * val).astype(BF16)

    y_ref[...] = x + jnp.dot(act_ref[...], wdown_ref[...], preferred_element_type=F32)


def _conv_ffn(x, gain, w_up, conv_w, conv_b, w_down, mixer_out=None, w_o=None):
    s = x.shape[0]
    const = lambda i: (0, 0)
    row_spec = pl.BlockSpec((FFN_ROW_TILE, D_MODEL), lambda i: (i, 0))
    mixer_proj = mixer_out is not None
    mixer_specs = [row_spec, pl.BlockSpec((D_MODEL, D_MODEL), const, pipeline_mode=pl.Buffered(1))]
    return pl.pallas_call(
        functools.partial(_conv_ffn_kernel, mixer_proj=mixer_proj),
        grid=(s // FFN_ROW_TILE,),
        in_specs=[row_spec] + (mixer_specs if mixer_proj else []) + [
            pl.BlockSpec((1, D_MODEL), const),
            pl.BlockSpec((D_MODEL, 2 * D_FF), const, pipeline_mode=pl.Buffered(1)),
            pl.BlockSpec((CONV_W, 2 * D_FF), const),
            pl.BlockSpec((1, 2 * D_FF), const),
            pl.BlockSpec((D_FF, D_MODEL), const, pipeline_mode=pl.Buffered(1)),
        ],
        out_specs=pl.BlockSpec((FFN_ROW_TILE, D_MODEL), lambda i: (i, 0)),
        out_shape=jax.ShapeDtypeStruct((s, D_MODEL), F32),
        scratch_shapes=[pltpu.VMEM((8, 2 * D_FF), F32), pltpu.VMEM((FFN_ROW_TILE, D_FF), BF16)],
        compiler_params=pltpu.CompilerParams(
            dimension_semantics=("arbitrary",), vmem_limit_bytes=VMEM_LIMIT),
        name="conv_ffn",
    )(x, *((mixer_out, w_o) if mixer_proj else ()), gain, w_up, conv_w, conv_b, w_down)


def _kv_proj_kernel(x_ref, g_ref, wt_ref, kg_ref, cos_ref, sin_ref, k_ref, vt_ref, km_ref, ksq_ref):
    nb = k_ref.shape[0]
    xn = _rms_rows(x_ref[...], g_ref[...]).astype(BF16)
    kt = _headnorm_rope_t(_nt_dot(wt_ref[:D_MODEL, :], xn), kg_ref[...], cos_ref[...], sin_ref[...], 1.0)
    kbt = kt.astype(BF16)
    k = kt.T.reshape(nb, MOBA_BLOCK, D_MODEL)
    km_ref[...] = jnp.mean(k, axis=1, keepdims=True)
    k_ref[...] = k.astype(BF16)
    kf = kbt.astype(F32)
    rows = []
    for b in range(nb):
        per_head = []
        for h in range(A_HEADS):
            kh = kf[h * A_DH:(h + 1) * A_DH, b * MOBA_BLOCK:(b + 1) * MOBA_BLOCK]
            n2 = jnp.max(jnp.sum(kh * kh, axis=0, keepdims=True), axis=1, keepdims=True)
            per_head.append(jnp.broadcast_to(n2, (1, A_DH)))
        rows.append(jnp.concatenate(per_head, axis=1)[None])
    ksq_ref[...] = jnp.concatenate(rows, axis=0)

    vt = _nt_dot(wt_ref[D_MODEL:, :], xn).astype(BF16)
    for b in range(nb):
        vt_ref[b] = vt[:, b * MOBA_BLOCK:(b + 1) * MOBA_BLOCK]


def _kv_proj(x, gain, wt_kv, k_gain, cos_t, sin_t):
    s = x.shape[0]
    nb = ROW_TILE // MOBA_BLOCK
    n_blocks = s // MOBA_BLOCK
    half = ROT_DIM // 2
    return pl.pallas_call(
        _kv_proj_kernel,
        grid=(s // ROW_TILE,),
        in_specs=[
            pl.BlockSpec((ROW_TILE, D_MODEL), lambda i: (i, 0)),
            pl.BlockSpec((1, D_MODEL), lambda i: (0, 0)),
            pl.BlockSpec((2 * D_MODEL, D_MODEL), lambda i: (0, 0), pipeline_mode=pl.Buffered(1)),
            pl.BlockSpec((A_DH, A_DH), lambda i: (0, 0)),
            pl.BlockSpec((half, ROW_TILE), lambda i: (0, i)),
            pl.BlockSpec((half, ROW_TILE), lambda i: (0, i)),
        ],
        out_specs=[
            pl.BlockSpec((nb, MOBA_BLOCK, D_MODEL), lambda i: (i, 0, 0)),
            pl.BlockSpec((nb, D_MODEL, MOBA_BLOCK), lambda i: (i, 0, 0)),
            pl.BlockSpec((nb, 1, D_MODEL), lambda i: (i, 0, 0)),
            pl.BlockSpec((nb, 1, D_MODEL), lambda i: (i, 0, 0)),
        ],
        out_shape=[
            jax.ShapeDtypeStruct((n_blocks, MOBA_BLOCK, D_MODEL), BF16),
            jax.ShapeDtypeStruct((n_blocks, D_MODEL, MOBA_BLOCK), BF16),
            jax.ShapeDtypeStruct((n_blocks, 1, D_MODEL), F32),
            jax.ShapeDtypeStruct((n_blocks, 1, D_MODEL), F32),
        ],
        compiler_params=pltpu.CompilerParams(
            dimension_semantics=("parallel",), vmem_limit_bytes=VMEM_LIMIT),
        name="kv_proj",
    )(x, gain, wt_kv, k_gain, cos_t, sin_t)


def _q_proj_kernel(x_ref, g_ref, wt_ref, qg_ref, cos_ref, sin_ref, qt_ref):
    xn = _rms_rows(x_ref[...], g_ref[...]).astype(BF16)
    qt = _headnorm_rope_t(_nt_dot(wt_ref[...], xn), qg_ref[...], cos_ref[...], sin_ref[...], A_DH ** -0.5 * LOG2_E)
    qt_ref[...] = qt.astype(BF16)


def _q_proj(x, gain, wt_q, q_gain, cos_t, sin_t):
    s = x.shape[0]
    half = ROT_DIM // 2
    return pl.pallas_call(
        _q_proj_kernel,
        grid=(s // ROW_TILE,),
        in_specs=[
            pl.BlockSpec((ROW_TILE, D_MODEL), lambda i: (i, 0)),
            pl.BlockSpec((1, D_MODEL), lambda i: (0, 0)),
            pl.BlockSpec((D_MODEL, D_MODEL), lambda i: (0, 0)),
            pl.BlockSpec((A_DH, A_DH), lambda i: (0, 0)),
            pl.BlockSpec((half, ROW_TILE), lambda i: (0, i)),
            pl.BlockSpec((half, ROW_TILE), lambda i: (0, i)),
        ],
        out_specs=pl.BlockSpec((D_MODEL, ROW_TILE), lambda i: (0, i)),
        out_shape=jax.ShapeDtypeStruct((D_MODEL, s), BF16),
        compiler_params=pltpu.CompilerParams(
            dimension_semantics=("parallel",), vmem_limit_bytes=VMEM_LIMIT),
        name="q_proj",
    )(x, gain, wt_q, q_gain, cos_t, sin_t)


def _moba_attn_kernel(qt_ref, k_ref, vt_ref, km_ref, ksq_ref, o_ref, sel_ref, bias_ref):
    nb = k_ref.shape[0]
    bs = MOBA_BLOCK
    cur = pl.program_id(1)
    kpos = lax.broadcasted_iota(jnp.int32, (bs, bs), 0)
    qpos = lax.broadcasted_iota(jnp.int32, (bs, bs), 1)
    causal = kpos <= qpos

    qts, own, gaps = [], [], []
    for hh in range(ATTN_HEADS):
        hs = slice(hh * A_DH, (hh + 1) * A_DH)
        qt = qt_ref[hs, :]
        qts.append(qt)
        gate = jnp.dot(km_ref[:, hs], qt, preferred_element_type=F32)
        blk = lax.broadcasted_iota(jnp.int32, gate.shape, 0)
        gate = jnp.where(blk < cur, gate, NEG_INF)
        sel = jnp.zeros(gate.shape, F32)
        for _ in range(MOBA_TOPK):
            top = jnp.max(gate, axis=0, keepdims=True)
            first = jnp.min(jnp.where((gate == top) & (gate > NEG_INF), blk, nb), axis=0, keepdims=True)
            pick = blk == first
            sel = jnp.where(pick, 1.0, sel)
            gate = jnp.where(pick, NEG_INF, gate)
        sel_ref[hh] = sel

        s_own = jnp.dot(k_ref[cur, :, hs], qt, preferred_element_type=F32)
        s_own = jnp.where(causal, s_own, NEG_INF)
        m0 = jnp.max(s_own, axis=0, keepdims=True)
        own.append((m0, jnp.exp2(s_own - m0)))
        bias_ref[hh] = jnp.where(sel > 0.0, -m0, MASKED_SCORE)

        qf = qt.astype(F32)
        q2 = jnp.sum(qf * qf, axis=0, keepdims=True)
        k2 = jnp.max(ksq_ref[:, hs], axis=0, keepdims=True)[:, 0:1]
        gaps.append(jnp.max(jnp.sqrt(q2 * k2) * NORM_BOUND_SLACK - m0))

    def finish(parts):
        o_ref[...] = jnp.concatenate([(acc / l).T for acc, l in parts], axis=1).astype(BF16)

    def fixed_max_path():
        ones = jnp.ones((ONES_ROWS, bs), BF16)
        init = []
        for hh in range(ATTN_HEADS):
            hs = slice(hh * A_DH, (hh + 1) * A_DH)
            vt_own = jnp.concatenate([vt_ref[cur, hs, :], ones], axis=0)
            init.append(jnp.dot(vt_own, own[hh][1].astype(BF16), preferred_element_type=F32))

        def make_body(unroll, base):
            def body(step, accs):
                j0 = base + step * unroll
                out = []
                for hh in range(ATTN_HEADS):
                    hs = slice(hh * A_DH, (hh + 1) * A_DH)
                    p_parts, vt_parts = [], []
                    for b in range(unroll):
                        s_b = jnp.dot(k_ref[j0 + b, :, hs], qts[hh], preferred_element_type=F32)
                        p_parts.append(jnp.exp2(s_b + bias_ref[hh, pl.ds(j0 + b, 1), :]).astype(BF16))
                        vt_parts.append(jnp.concatenate([vt_ref[j0 + b, hs, :], ones], axis=0))
                    out.append(accs[hh] + jnp.dot(jnp.concatenate(vt_parts, axis=1),
                                                  jnp.concatenate(p_parts, axis=0), preferred_element_type=F32))
                return tuple(out)
            return body

        n_main = cur // FAST_UNROLL
        accs = lax.fori_loop(0, n_main, make_body(FAST_UNROLL, 0), tuple(init))
        done = n_main * FAST_UNROLL
        accs = lax.fori_loop(0, (cur - done + TAIL_UNROLL - 1) // TAIL_UNROLL, make_body(TAIL_UNROLL, done), accs)
        finish([(acc[:A_DH], acc[A_DH:A_DH + 1]) for acc in accs])

    def running_max_path():
        init = []
        for hh in range(ATTN_HEADS):
            hs = slice(hh * A_DH, (hh + 1) * A_DH)
            m0, p0 = own[hh]
            init.append((m0, jnp.sum(p0, axis=0, keepdims=True),
                         jnp.dot(vt_ref[cur, hs, :], p0.astype(BF16), preferred_element_type=F32)))

        def body(step, carry):
            j0 = step * ATTN_UNROLL
            out = []
            for hh in range(ATTN_HEADS):
                hs = slice(hh * A_DH, (hh + 1) * A_DH)
                m, l, acc = carry[hh]
                s_parts, vt_parts = [], []
                for b in range(ATTN_UNROLL):
                    s_b = jnp.dot(k_ref[j0 + b, :, hs], qts[hh], preferred_element_type=F32)
                    s_parts.append(jnp.where(sel_ref[hh, pl.ds(j0 + b, 1), :] > 0.0, s_b, NEG_INF))
                    vt_parts.append(vt_ref[j0 + b, hs, :])
                s_t = jnp.concatenate(s_parts, axis=0)
                m_new = jnp.maximum(m, jnp.max(s_t, axis=0, keepdims=True))
                alpha = jnp.exp2(m - m_new)
                p = jnp.exp2(s_t - m_new)
                l = alpha * l + jnp.sum(p, axis=0, keepdims=True)
                acc = alpha * acc + jnp.dot(jnp.concatenate(vt_parts, axis=1), p.astype(BF16),
                                            preferred_element_type=F32)
                out.append((m_new, l, acc))
            return tuple(out)

        final = lax.fori_loop(0, (cur + ATTN_UNROLL - 1) // ATTN_UNROLL, body, tuple(init))
        finish([(acc, l) for _, l, acc in final])

    worst_gap = functools.reduce(jnp.maximum, gaps)
    lax.cond(worst_gap <= MAX_EXP2_ARG, fixed_max_path, running_max_path)


def _moba_attn(qt, kb, vtb, kmean, ksq):
    s = qt.shape[1]
    nb = s // MOBA_BLOCK
    gw = ATTN_HEADS * A_DH
    return pl.pallas_call(
        _moba_attn_kernel,
        grid=(A_HEADS // ATTN_HEADS, nb),
        in_specs=[
            pl.BlockSpec((gw, MOBA_BLOCK), lambda g, i: (g, i)),
            pl.BlockSpec((nb, MOBA_BLOCK, gw), lambda g, i: (0, 0, g), pipeline_mode=pl.Buffered(1)),
            pl.BlockSpec((nb, gw, MOBA_BLOCK), lambda g, i: (0, g, 0), pipeline_mode=pl.Buffered(1)),
            pl.BlockSpec((nb, gw), lambda g, i: (0, g)),
            pl.BlockSpec((nb, gw), lambda g, i: (0, g)),
        ],
        out_specs=pl.BlockSpec((MOBA_BLOCK, gw), lambda g, i: (i, g)),
        out_shape=jax.ShapeDtypeStruct((s, D_MODEL), BF16),
        scratch_shapes=[pltpu.VMEM((ATTN_HEADS, nb, MOBA_BLOCK), F32),
                        pltpu.VMEM((ATTN_HEADS, nb, MOBA_BLOCK), F32)],
        compiler_params=pltpu.CompilerParams(
            dimension_semantics=("parallel", "arbitrary"), vmem_limit_bytes=VMEM_LIMIT),
        name="moba_attn",
    )(qt, kb, vtb, kmean, ksq)


def _rope_tables_t(s_len):
    pos = jnp.arange(s_len, dtype=F32)
    inv = ROPE_THETA ** (-jnp.arange(0, ROT_DIM, 2, dtype=F32) / ROT_DIM)
    ang = inv[:, None] * pos[None, :]
    return jnp.cos(ang), jnp.sin(ang)


def _lane_replicated(gain):
    return jnp.broadcast_to(gain[:, None], (gain.shape[0], 128))


def kernel(x, a_norm, a_w_in, a_b_gates, a_h_norm, a_w_out, kv_norm, w_kv, k_norm, b_norm, b_w_q, b_q_norm,
           b_w_o, f_norm, f_w_up, f_conv_w, f_conv_b, f_w_down):
    b_, s_, d_ = x.shape
    assert (b_, d_) == (1, D_MODEL) and s_ % ROW_TILE == 0 and s_ % MOBA_BLOCK == 0
    cos_t, sin_t = _rope_tables_t(s_)
    h = x.reshape(s_, d_)
    kb = vtb = kmean = ksq = None
    for l in range(DEPTH):
        mixer = ()
        if l < N_A:
            w_in = a_w_in[l]
            wg = w_in[:, M_QKVO:]
            wg_hi = wg.astype(BF16)
            wg_lo = (wg - wg_hi.astype(F32)).astype(BF16)
            wg_cols = jnp.concatenate([wg_hi, wg_lo, jnp.zeros((d_, 128 - 4 * M_HEADS), BF16)], axis=1)
            dk = M_HEADS * M_DQK
            wt_voq = jnp.concatenate([w_in[:, 2 * dk:M_QKVO], w_in[:, :dk]], axis=1).T.astype(BF16)
            k, ut, gates_t = _mlstm_in_proj(h, a_norm[l][None], w_in[:, dk:2 * dk].astype(BF16), wt_voq, wg_cols)
            h = _mlstm_core(h, k, ut, gates_t, a_b_gates[l].reshape(2 * M_HEADS, 1),
                            _lane_replicated(a_h_norm[l]), a_w_out[l].T.astype(BF16))
        else:
            j = l - N_A
            qt = _q_proj(h, b_norm[j][None], b_w_q[j].T.astype(BF16), _lane_replicated(b_q_norm[j]), cos_t, sin_t)
            mixer = (_moba_attn(qt, kb, vtb, kmean, ksq), b_w_o[j].astype(BF16))
        h = _conv_ffn(h, f_norm[l][None], f_w_up[l].astype(BF16), f_conv_w[l], f_conv_b[l][None],
                      f_w_down[l].astype(BF16), *mixer)
        if l == N_A - 1:
            kb, vtb, km, ks = _kv_proj(h, kv_norm[None], w_kv.T.astype(BF16), _lane_replicated(k_norm), cos_t, sin_t)
            kmean = km.reshape(s_ // MOBA_BLOCK, D_MODEL).astype(BF16)
            ksq = ks.reshape(s_ // MOBA_BLOCK, D_MODEL)
    return h.reshape(b_, s_, d_)
```

```python
import functools

import jax
import jax.numpy as jnp
from jax import lax
from jax.experimental import pallas as pl
from jax.experimental.pallas import tpu as pltpu

F32 = jnp.float32
BF16 = jnp.bfloat16

D_MODEL = 1024
DEPTH = 4
N_A = DEPTH // 2
M_HEADS = 4
M_DQK = 128
M_DV = D_MODEL // M_HEADS
M_QKVO = 2 * M_HEADS * M_DQK + 2 * M_HEADS * M_DV
M_UT_Q = 2 * M_HEADS * M_DV
M_UT_ROWS = M_UT_Q + M_HEADS * M_DQK
A_HEADS = 8
A_DH = D_MODEL // A_HEADS
ROT_DIM = A_DH // 4
ROPE_THETA = 500000.0
MOBA_BLOCK = 256
MOBA_TOPK = 3
D_FF = 2816
CONV_W = 3
EPS = 1e-6

M_CHUNK = 256
ROW_TILE = 512
FFN_ROW_TILE = 1024
FFN_CHUNK = 256
ATTN_HEADS = 4
ATTN_UNROLL = 2
FAST_UNROLL = 8
TAIL_UNROLL = 4
LOG2_E = 1.4426950408889634
MASKED_SCORE = -1e30
MAX_EXP2_ARG = 100.0
NORM_BOUND_SLACK = 1.0 + 2.0 ** -6
VMEM_LIMIT = 56 * 1024 * 1024
NEG_INF = float("-inf")


def _rms_rows(x, gain):
    return x * lax.rsqrt(jnp.mean(x * x, axis=-1, keepdims=True) + EPS) * gain


def _nt_dot(a, b, **kw):
    return lax.dot_general(a, b, (((1,), (1,)), ((), ())), preferred_element_type=F32, **kw)


def _headnorm_rope_t(t, gain, cos, sin, scale):
    half = ROT_DIM // 2
    g = jnp.concatenate([gain] * (t.shape[1] // gain.shape[1]), axis=1)
    outs = []
    for h in range(t.shape[0] // A_DH):
        th = t[h * A_DH:(h + 1) * A_DH, :]
        th = th * lax.rsqrt(jnp.mean(th * th, axis=0, keepdims=True) + EPS) * g
        t1, t2 = th[:half], th[half:ROT_DIM]
        th = jnp.concatenate([t1 * cos - t2 * sin, t2 * cos + t1 * sin, th[ROT_DIM:]], axis=0)
        outs.append(th * scale if scale != 1.0 else th)
    return jnp.concatenate(outs, axis=0)


def _mlstm_in_kernel(x_ref, g_ref, wk_ref, wt_ref, bias_ref, tri_ref, k_ref, ut_ref, gt_ref):
    xn = _rms_rows(x_ref[...], g_ref[...])
    xh = xn.astype(BF16)
    k_ref[...] = jnp.dot(xh, wk_ref[...], preferred_element_type=F32).astype(BF16)
    ut = _nt_dot(wt_ref[...], xh)
    ut_ref[:M_UT_Q, :] = ut[:M_UT_Q].astype(BF16)
    ut_ref[M_UT_Q:, :] = (ut[M_UT_Q:M_UT_ROWS] * M_DQK ** -0.5).astype(BF16)
    xl = (xn - xh.astype(F32)).astype(BF16)
    e = ut[M_UT_ROWS:] + _nt_dot(wt_ref[M_UT_ROWS:, :], xl)
    g = e[:2 * M_HEADS] + e[2 * M_HEADS:] + bias_ref[...]
    fpre = g[M_HEADS:]
    lf = jnp.minimum(fpre, 0.0) - jnp.log1p(jnp.exp(-jnp.abs(fpre)))
    lf8 = jnp.concatenate([lf, jnp.zeros_like(lf)], axis=0)
    hi = lf8.astype(BF16).astype(F32)
    mid = (lf8 - hi).astype(BF16).astype(F32)
    lo = lf8 - hi - mid
    pieces = jnp.concatenate([hi, mid, lo, jnp.zeros_like(hi)], axis=0).astype(BF16)
    c = jnp.dot(pieces, tri_ref[...], preferred_element_type=F32)
    rows = 2 * M_HEADS
    b = c[:rows] + c[rows:2 * rows] + c[2 * rows:3 * rows]
    gt_ref[...] = jnp.concatenate([g[:M_HEADS], b[:M_HEADS]], axis=0)


def _mlstm_in_proj(x, gain, w_k, wt_voqg, gate_bias):
    s = x.shape[0]
    dk = M_HEADS * M_DQK
    pos = jnp.arange(ROW_TILE)
    tri = ((pos[:, None] <= pos[None, :]) & (pos[:, None] // M_CHUNK == pos[None, :] // M_CHUNK)).astype(BF16)
    return pl.pallas_call(
        _mlstm_in_kernel,
        grid=(s // ROW_TILE,),
        in_specs=[
            pl.BlockSpec((ROW_TILE, D_MODEL), lambda i: (i, 0)),
            pl.BlockSpec((1, D_MODEL), lambda i: (0, 0)),
            pl.BlockSpec((D_MODEL, dk), lambda i: (0, 0), pipeline_mode=pl.Buffered(1)),
            pl.BlockSpec((M_UT_ROWS + 4 * M_HEADS, D_MODEL), lambda i: (0, 0), pipeline_mode=pl.Buffered(1)),
            pl.BlockSpec((2 * M_HEADS, 1), lambda i: (0, 0)),
            pl.BlockSpec((ROW_TILE, ROW_TILE), lambda i: (0, 0)),
        ],
        out_specs=[
            pl.BlockSpec((ROW_TILE, dk), lambda i: (i, 0)),
            pl.BlockSpec((M_UT_ROWS, ROW_TILE), lambda i: (0, i)),
            pl.BlockSpec((2 * M_HEADS, ROW_TILE), lambda i: (0, i)),
        ],
        out_shape=[jax.ShapeDtypeStruct((s, dk), BF16), jax.ShapeDtypeStruct((M_UT_ROWS, s), BF16),
                   jax.ShapeDtypeStruct((2 * M_HEADS, s), F32)],
        compiler_params=pltpu.CompilerParams(
            dimension_semantics=("parallel",), vmem_limit_bytes=VMEM_LIMIT),
        name="mlstm_in_proj",
    )(x, gain, w_k, wt_voqg, gate_bias, tri)


def _mlstm_core_kernel(x_ref, k_ref, vt_ref, ot_ref, qt_ref, gt_ref, hg_ref, woutt_ref,
                       y_ref, ct_ref, n_ref, m_ref):
    L = M_CHUNK

    @pl.when(pl.program_id(0) == 0)
    def _():
        ct_ref[...] = jnp.zeros_like(ct_ref)
        n_ref[...] = jnp.zeros_like(n_ref)
        m_ref[...] = jnp.zeros_like(m_ref)

    g = gt_ref[...]
    gi, b_rows = g[:M_HEADS], g[M_HEADS:]
    s_i = lax.broadcasted_iota(jnp.int32, (L, L), 0)
    t_i = lax.broadcasted_iota(jnp.int32, (L, L), 1)
    causal = s_i <= t_i
    a_rows = gi - b_rows
    a_cols = jnp.concatenate([a_rows, jnp.zeros((128 - M_HEADS, L), F32)], axis=0).T
    hgain = jnp.concatenate([hg_ref[...]] * (L // 128), axis=1)

    heads = []
    for h in range(M_HEADS):
        qt = qt_ref[h * M_DQK:(h + 1) * M_DQK, :]
        kh = k_ref[:, h * M_DQK:(h + 1) * M_DQK]
        vt = vt_ref[h * M_DV:(h + 1) * M_DV, :]
        b_row = b_rows[h:h + 1, :]
        i_row = gi[h:h + 1, :]
        m_prev = m_ref[h:h + 1, 0:1]
        ct_prev = ct_ref[h]
        n_prev = n_ref[h]

        d = jnp.where(causal, a_cols[:, h:h + 1] + b_row, NEG_INF)
        inter = b_row + m_prev
        m_t = jnp.maximum(inter, jnp.max(d, axis=0, keepdims=True))
        w_inter = jnp.exp(inter - m_t)
        s_mat = jnp.dot(kh, qt, preferred_element_type=F32) * jnp.exp(d - m_t)
        num = w_inter * jnp.dot(ct_prev.astype(BF16), qt, preferred_element_type=F32) \
            + jnp.dot(vt, s_mat.astype(BF16), preferred_element_type=F32)
        qn = jnp.dot(n_prev.astype(BF16), qt, preferred_element_type=F32)[0:1, :]
        den = w_inter * qn + jnp.sum(s_mat, axis=0, keepdims=True)
        hh = num / jnp.maximum(jnp.abs(den), jnp.exp(-m_t))

        b_last = b_row[:, L - 1:L]
        g_row = b_last - b_row + i_row
        m_new = jnp.maximum(b_last + m_prev, jnp.max(g_row, axis=-1, keepdims=True))
        decay = jnp.exp(b_last + m_prev - m_new)
        w_row = jnp.exp(g_row - m_new)
        ct_ref[h] = decay * ct_prev + jnp.dot((vt.astype(F32) * w_row).astype(BF16), kh,
                                              preferred_element_type=F32)
        n_ref[h] = decay * n_prev + jnp.dot(jnp.broadcast_to(w_row, (8, L)).astype(BF16), kh,
                                            preferred_element_type=F32)
        m_ref[h:h + 1, :] = jnp.broadcast_to(m_new, (1, 128))

        hs = slice(h * M_DV, (h + 1) * M_DV)
        hn = hh * lax.rsqrt(jnp.mean(hh * hh, axis=0, keepdims=True) + EPS) * hgain[hs]
        heads.append((hn * jax.nn.sigmoid(ot_ref[hs, :].astype(F32))).astype(BF16))

    yt = jnp.dot(woutt_ref[...], jnp.concatenate(heads, axis=0), preferred_element_type=F32)
    y_ref[...] = x_ref[...] + yt.T


def _mlstm_core(x, k, ut, gates_t, h_gain, wt_out):
    s = x.shape[0]
    L = M_CHUNK
    dk, dv = M_HEADS * M_DQK, M_HEADS * M_DV
    return pl.pallas_call(
        _mlstm_core_kernel,
        grid=(s // L,),
        in_specs=[
            pl.BlockSpec((L, D_MODEL), lambda c: (c, 0)),
            pl.BlockSpec((L, dk), lambda c: (c, 0)),
            pl.BlockSpec((dv, L), lambda c: (0, c)),
            pl.BlockSpec((dv, L), lambda c: (1, c)),
            pl.BlockSpec((dk, L), lambda c: (M_UT_Q // dk, c)),
            pl.BlockSpec((2 * M_HEADS, L), lambda c: (0, c)),
            pl.BlockSpec((dv, 128), lambda c: (0, 0)),
            pl.BlockSpec((D_MODEL, D_MODEL), lambda c: (0, 0)),
        ],
        out_specs=pl.BlockSpec((L, D_MODEL), lambda c: (c, 0)),
        out_shape=jax.ShapeDtypeStruct((s, D_MODEL), F32),
        scratch_shapes=[
            pltpu.VMEM((M_HEADS, M_DV, M_DQK), F32),
            pltpu.VMEM((M_HEADS, 8, M_DQK), F32),
            pltpu.VMEM((8, 128), F32),
        ],
        compiler_params=pltpu.CompilerParams(
            dimension_semantics=("arbitrary",), vmem_limit_bytes=VMEM_LIMIT),
        name="mlstm_core",
    )(x, k, ut, ut, ut, gates_t, h_gain, wt_out)


def _conv_ffn_kernel(*refs, mixer_proj):
    if mixer_proj:
        x_ref, a_ref, wo_ref, g_ref, wup_ref, cw_ref, cb_ref, wdown_ref, y_ref, carry_ref, act_ref = refs
    else:
        x_ref, g_ref, wup_ref, cw_ref, cb_ref, wdown_ref, y_ref, carry_ref, act_ref = refs
    tm = x_ref.shape[0]

    @pl.when(pl.program_id(0) == 0)
    def _():
        carry_ref[...] = jnp.zeros_like(carry_ref)

    x = x_ref[...]
    if mixer_proj:
        x = x + jnp.dot(a_ref[...], wo_ref[...], preferred_element_type=F32)
    xn = _rms_rows(x, g_ref[...]).astype(BF16)
    top = lax.broadcasted_iota(jnp.int32, (8, FFN_CHUNK), 0)

    def conv_cols(col):
        u = jnp.dot(xn, wup_ref[:, col:col + FFN_CHUNK], preferred_element_type=F32)
        prev = carry_ref[:, col:col + FFN_CHUNK]
        u1 = pltpu.roll(u, 1, 0)
        u2 = pltpu.roll(u, 2, 0)
        u1_top = jnp.where(top == 0, prev[7:8], u1[0:8])
        u2_top = jnp.where(top == 0, prev[6:7], jnp.where(top == 1, prev[7:8], u2[0:8]))
        u1 = jnp.concatenate([u1_top, u1[8:]], axis=0)
        u2 = jnp.concatenate([u2_top, u2[8:]], axis=0)
        carry_ref[:, col:col + FFN_CHUNK] = u[tm - 8:tm]
        cw = cw_ref[:, col:col + FFN_CHUNK]
        return cb_ref[:, col:col + FFN_CHUNK] + u2 * cw[0:1] + u1 * cw[1:2] + u * cw[2:3]

    for c in range(D_FF // FFN_CHUNK):
        val = conv_cols(c * FFN_CHUNK)
        gate = conv_cols(D_FF + c * FFN_CHUNK)
        act_ref[:, c * FFN_CHUNK:(c + 1) * FFN_CHUNK] = (gate * jax.nn.sigmoid(gate) * val).astype(BF16)

    y_ref[...] = x + jnp.dot(act_ref[...], wdown_ref[...], preferred_element_type=F32)


def _conv_ffn(x, gain, w_up, conv_w, conv_b, w_down, mixer_out=None, w_o=None):
    s = x.shape[0]
    const = lambda i: (0, 0)
    row_spec = pl.BlockSpec((FFN_ROW_TILE, D_MODEL), lambda i: (i, 0))
    mixer_proj = mixer_out is not None
    mixer_specs = [row_spec, pl.BlockSpec((D_MODEL, D_MODEL), const, pipeline_mode=pl.Buffered(1))]
    return pl.pallas_call(
        functools.partial(_conv_ffn_kernel, mixer_proj=mixer_proj),
        grid=(s // FFN_ROW_TILE,),
        in_specs=[row_spec] + (mixer_specs if mixer_proj else []) + [
            pl.BlockSpec((1, D_MODEL), const),
            pl.BlockSpec((D_MODEL, 2 * D_FF), const, pipeline_mode=pl.Buffered(1)),
            pl.BlockSpec((CONV_W, 2 * D_FF), const),
            pl.BlockSpec((1, 2 * D_FF), const),
            pl.BlockSpec((D_FF, D_MODEL), const, pipeline_mode=pl.Buffered(1)),
        ],
        out_specs=pl.BlockSpec((FFN_ROW_TILE, D_MODEL), lambda i: (i, 0)),
        out_shape=jax.ShapeDtypeStruct((s, D_MODEL), F32),
        scratch_shapes=[pltpu.VMEM((8, 2 * D_FF), F32), pltpu.VMEM((FFN_ROW_TILE, D_FF), BF16)],
        compiler_params=pltpu.CompilerParams(
            dimension_semantics=("arbitrary",), vmem_limit_bytes=VMEM_LIMIT),
        name="conv_ffn",
    )(x, *((mixer_out, w_o) if mixer_proj else ()), gain, w_up, conv_w, conv_b, w_down)


def _kv_proj_kernel(x_ref, g_ref, wt_ref, kg_ref, cos_ref, sin_ref, k_ref, vt_ref, km_ref, ksq_ref):
    nb = k_ref.shape[0]
    xn = _rms_rows(x_ref[...], g_ref[...]).astype(BF16)
    kt = _headnorm_rope_t(_nt_dot(wt_ref[:D_MODEL, :], xn), kg_ref[...], cos_ref[...], sin_ref[...], 1.0)
    kbt = kt.astype(BF16)
    k = kt.T.reshape(nb, MOBA_BLOCK, D_MODEL)
    km_ref[...] = jnp.mean(k, axis=1, keepdims=True)
    k_ref[...] = k.astype(BF16)
    kf = kbt.astype(F32)
    rows = []
    for b in range(nb):
        per_head = []
        for h in range(A_HEADS):
            kh = kf[h * A_DH:(h + 1) * A_DH, b * MOBA_BLOCK:(b + 1) * MOBA_BLOCK]
            n2 = jnp.max(jnp.sum(kh * kh, axis=0, keepdims=True), axis=1, keepdims=True)
            per_head.append(jnp.broadcast_to(n2, (1, A_DH)))
        rows.append(jnp.concatenate(per_head, axis=1)[None])
    ksq_ref[...] = jnp.concatenate(rows, axis=0)

    vt = _nt_dot(wt_ref[D_MODEL:, :], xn).astype(BF16)
    for b in range(nb):
        vt_ref[b] = vt[:, b * MOBA_BLOCK:(b + 1) * MOBA_BLOCK]


def _kv_proj(x, gain, wt_kv, k_gain, cos_t, sin_t):
    s = x.shape[0]
    nb = ROW_TILE // MOBA_BLOCK
    n_blocks = s // MOBA_BLOCK
    half = ROT_DIM // 2
    return pl.pallas_call(
        _kv_proj_kernel,
        grid=(s // ROW_TILE,),
        in_specs=[
            pl.BlockSpec((ROW_TILE, D_MODEL), lambda i: (i, 0)),
            pl.BlockSpec((1, D_MODEL), lambda i: (0, 0)),
            pl.BlockSpec((2 * D_MODEL, D_MODEL), lambda i: (0, 0), pipeline_mode=pl.Buffered(1)),
            pl.BlockSpec((A_DH, A_DH), lambda i: (0, 0)),
            pl.BlockSpec((half, ROW_TILE), lambda i: (0, i)),
            pl.BlockSpec((half, ROW_TILE), lambda i: (0, i)),
        ],
        out_specs=[
            pl.BlockSpec((nb, MOBA_BLOCK, D_MODEL), lambda i: (i, 0, 0)),
            pl.BlockSpec((nb, D_MODEL, MOBA_BLOCK), lambda i: (i, 0, 0)),
            pl.BlockSpec((nb, 1, D_MODEL), lambda i: (i, 0, 0)),
            pl.BlockSpec((nb, 1, D_MODEL), lambda i: (i, 0, 0)),
        ],
        out_shape=[
            jax.ShapeDtypeStruct((n_blocks, MOBA_BLOCK, D_MODEL), BF16),
            jax.ShapeDtypeStruct((n_blocks, D_MODEL, MOBA_BLOCK), BF16),
            jax.ShapeDtypeStruct((n_blocks, 1, D_MODEL), F32),
            jax.ShapeDtypeStruct((n_blocks, 1, D_MODEL), F32),
        ],
        compiler_params=pltpu.CompilerParams(
            dimension_semantics=("parallel",), vmem_limit_bytes=VMEM_LIMIT),
        name="kv_proj",
    )(x, gain, wt_kv, k_gain, cos_t, sin_t)


def _q_proj_kernel(x_ref, g_ref, wt_ref, qg_ref, cos_ref, sin_ref, qt_ref):
    xn = _rms_rows(x_ref[...], g_ref[...]).astype(BF16)
    qt = _headnorm_rope_t(_nt_dot(wt_ref[...], xn), qg_ref[...], cos_ref[...], sin_ref[...], A_DH ** -0.5 * LOG2_E)
    qt_ref[...] = qt.astype(BF16)


def _q_proj(x, gain, wt_q, q_gain, cos_t, sin_t):
    s = x.shape[0]
    half = ROT_DIM // 2
    return pl.pallas_call(
        _q_proj_kernel,
        grid=(s // ROW_TILE,),
        in_specs=[
            pl.BlockSpec((ROW_TILE, D_MODEL), lambda i: (i, 0)),
            pl.BlockSpec((1, D_MODEL), lambda i: (0, 0)),
            pl.BlockSpec((D_MODEL, D_MODEL), lambda i: (0, 0)),
            pl.BlockSpec((A_DH, A_DH), lambda i: (0, 0)),
            pl.BlockSpec((half, ROW_TILE), lambda i: (0, i)),
            pl.BlockSpec((half, ROW_TILE), lambda i: (0, i)),
        ],
        out_specs=pl.BlockSpec((D_MODEL, ROW_TILE), lambda i: (0, i)),
        out_shape=jax.ShapeDtypeStruct((D_MODEL, s), BF16),
        compiler_params=pltpu.CompilerParams(
            dimension_semantics=("parallel",), vmem_limit_bytes=VMEM_LIMIT),
        name="q_proj",
    )(x, gain, wt_q, q_gain, cos_t, sin_t)


def _moba_attn_kernel(qt_ref, k_ref, vt_ref, km_ref, ksq_ref, o_ref, sel_ref, bias_ref):
    nb = k_ref.shape[0]
    bs = MOBA_BLOCK
    cur = pl.program_id(1)
    kpos = lax.broadcasted_iota(jnp.int32, (bs, bs), 0)
    qpos = lax.broadcasted_iota(jnp.int32, (bs, bs), 1)
    causal = kpos <= qpos

    qts, own, gaps = [], [], []
    for hh in range(ATTN_HEADS):
        hs = slice(hh * A_DH, (hh + 1) * A_DH)
        qt = qt_ref[hs, :]
        qts.append(qt)
        gate = jnp.dot(km_ref[:, hs], qt, preferred_element_type=F32)
        blk = lax.broadcasted_iota(jnp.int32, gate.shape, 0)
        gate = jnp.where(blk < cur, gate, NEG_INF)
        sel = jnp.zeros(gate.shape, F32)
        for _ in range(MOBA_TOPK):
            top = jnp.max(gate, axis=0, keepdims=True)
            first = jnp.min(jnp.where(gate == top, blk, nb), axis=0, keepdims=True)
            first = jnp.where(top > NEG_INF, first, nb)
            pick = blk == first
            sel = jnp.where(pick, 1.0, sel)
            gate = jnp.where(pick, NEG_INF, gate)
        sel_ref[hh] = sel

        s_own = jnp.dot(k_ref[cur, :, hs], qt, preferred_element_type=F32)
        s_own = jnp.where(causal, s_own, NEG_INF)
        m0 = jnp.max(s_own, axis=0, keepdims=True)
        own.append((m0, jnp.exp2(s_own - m0)))
        bias_ref[hh] = jnp.where(sel > 0.0, -m0, MASKED_SCORE)

        qf = qt.astype(F32)
        q2 = jnp.sum(qf * qf, axis=0, keepdims=True)
        k2 = jnp.max(ksq_ref[:, hs], axis=0, keepdims=True)[:, 0:1]
        gaps.append(jnp.sqrt(q2 * k2) * NORM_BOUND_SLACK - m0)

    def finish(parts):
        o_ref[...] = jnp.concatenate([(acc / l).T for acc, l in parts], axis=1).astype(BF16)

    def fixed_max_path():
        def sublane_partial(p):
            return jnp.sum(p.reshape(p.shape[0] // 8, 8, p.shape[1]), axis=0)

        init = []
        for hh in range(ATTN_HEADS):
            hs = slice(hh * A_DH, (hh + 1) * A_DH)
            p0 = own[hh][1]
            init.append((jnp.dot(vt_ref[cur, hs, :], p0.astype(BF16), preferred_element_type=F32),
                         sublane_partial(p0)))

        def make_body(unroll, base):
            def body(step, carry):
                j0 = base + step * unroll
                out = []
                for hh in range(ATTN_HEADS):
                    hs = slice(hh * A_DH, (hh + 1) * A_DH)
                    acc, lsum = carry[hh]
                    p_parts, vt_parts = [], []
                    for b in range(unroll):
                        s_b = jnp.dot(k_ref[j0 + b, :, hs], qts[hh], preferred_element_type=F32)
                        p_b = jnp.exp2(s_b + bias_ref[hh, pl.ds(j0 + b, 1), :])
                        lsum = lsum + sublane_partial(p_b)
                        p_parts.append(p_b.astype(BF16))
                        vt_parts.append(vt_ref[j0 + b, hs, :])
                    acc = acc + jnp.dot(jnp.concatenate(vt_parts, axis=1), jnp.concatenate(p_parts, axis=0),
                                        preferred_element_type=F32)
                    out.append((acc, lsum))
                return tuple(out)
            return body

        n_main = cur // FAST_UNROLL
        carry = lax.fori_loop(0, n_main, make_body(FAST_UNROLL, 0), tuple(init))
        done = n_main * FAST_UNROLL
        carry = lax.fori_loop(0, (cur - done + TAIL_UNROLL - 1) // TAIL_UNROLL, make_body(TAIL_UNROLL, done), carry)
        finish([(acc, jnp.sum(lsum, axis=0, keepdims=True)) for acc, lsum in carry])

    def running_max_path():
        init = []
        for hh in range(ATTN_HEADS):
            hs = slice(hh * A_DH, (hh + 1) * A_DH)
            m0, p0 = own[hh]
            init.append((m0, jnp.sum(p0, axis=0, keepdims=True),
                         jnp.dot(vt_ref[cur, hs, :], p0.astype(BF16), preferred_element_type=F32)))

        def body(step, carry):
            j0 = step * ATTN_UNROLL
            out = []
            for hh in range(ATTN_HEADS):
                hs = slice(hh * A_DH, (hh + 1) * A_DH)
                m, l, acc = carry[hh]
                s_parts, vt_parts = [], []
                for b in range(ATTN_UNROLL):
                    s_b = jnp.dot(k_ref[j0 + b, :, hs], qts[hh], preferred_element_type=F32)
                    s_parts.append(jnp.where(sel_ref[hh, pl.ds(j0 + b, 1), :] > 0.0, s_b, NEG_INF))
                    vt_parts.append(vt_ref[j0 + b, hs, :])
                s_t = jnp.concatenate(s_parts, axis=0)
                m_new = jnp.maximum(m, jnp.max(s_t, axis=0, keepdims=True))
                alpha = jnp.exp2(m - m_new)
                p = jnp.exp2(s_t - m_new)
                l = alpha * l + jnp.sum(p, axis=0, keepdims=True)
                acc = alpha * acc + jnp.dot(jnp.concatenate(vt_parts, axis=1), p.astype(BF16),
                                            preferred_element_type=F32)
                out.append((m_new, l, acc))
            return tuple(out)

        final = lax.fori_loop(0, (cur + ATTN_UNROLL - 1) // ATTN_UNROLL, body, tuple(init))
        finish([(acc, l) for _, l, acc in final])

    worst_gap = jnp.max(functools.reduce(jnp.maximum, gaps))
    lax.cond(worst_gap <= MAX_EXP2_ARG, fixed_max_path, running_max_path)


def _moba_attn(qt, kb, vtb, kmean, ksq):
    s = qt.shape[1]
    nb = s // MOBA_BLOCK
    gw = ATTN_HEADS * A_DH
    return pl.pallas_call(
        _moba_attn_kernel,
        grid=(A_HEADS // ATTN_HEADS, nb),
        in_specs=[
            pl.BlockSpec((gw, MOBA_BLOCK), lambda g, i: (g, i)),
            pl.BlockSpec((nb, MOBA_BLOCK, gw), lambda g, i: (0, 0, g), pipeline_mode=pl.Buffered(1)),
            pl.BlockSpec((nb, gw, MOBA_BLOCK), lambda g, i: (0, g, 0), pipeline_mode=pl.Buffered(1)),
            pl.BlockSpec((nb, gw), lambda g, i: (0, g)),
            pl.BlockSpec((nb, gw), lambda g, i: (0, g)),
        ],
        out_specs=pl.BlockSpec((MOBA_BLOCK, gw), lambda g, i: (i, g)),
        out_shape=jax.ShapeDtypeStruct((s, D_MODEL), BF16),
        scratch_shapes=[pltpu.VMEM((ATTN_HEADS, nb, MOBA_BLOCK), F32),
                        pltpu.VMEM((ATTN_HEADS, nb, MOBA_BLOCK), F32)],
        compiler_params=pltpu.CompilerParams(
            dimension_semantics=("parallel", "arbitrary"), vmem_limit_bytes=VMEM_LIMIT),
        name="moba_attn",
    )(qt, kb, vtb, kmean, ksq)


def _rope_tables_t(s_len):
    pos = jnp.arange(s_len, dtype=F32)
    inv = ROPE_THETA ** (-jnp.arange(0, ROT_DIM, 2, dtype=F32) / ROT_DIM)
    ang = inv[:, None] * pos[None, :]
    return jnp.cos(ang), jnp.sin(ang)


def _lane_replicated(gain):
    return jnp.broadcast_to(gain[:, None], (gain.shape[0], 128))


def kernel(x, a_norm, a_w_in, a_b_gates, a_h_norm, a_w_out, kv_norm, w_kv, k_norm, b_norm, b_w_q, b_q_norm,
           b_w_o, f_norm, f_w_up, f_conv_w, f_conv_b, f_w_down):
    b_, s_, d_ = x.shape
    assert (b_, d_) == (1, D_MODEL) and s_ % ROW_TILE == 0 and s_ % MOBA_BLOCK == 0
    cos_t, sin_t = _rope_tables_t(s_)
    h = x.reshape(s_, d_)
    kb = vtb = kmean = ksq = None
    for l in range(DEPTH):
        mixer = ()
        if l < N_A:
            w_in = a_w_in[l]
            wg = w_in[:, M_QKVO:]
            wg_hi = wg.astype(BF16)
            wg_lo = (wg - wg_hi.astype(F32)).astype(BF16)
            dk = M_HEADS * M_DQK
            wt_voqg = jnp.concatenate([w_in[:, 2 * dk:M_QKVO].astype(BF16), w_in[:, :dk].astype(BF16),
                                       wg_hi, wg_lo], axis=1).T
            k, ut, gates_t = _mlstm_in_proj(h, a_norm[l][None], w_in[:, dk:2 * dk].astype(BF16), wt_voqg,
                                            a_b_gates[l].reshape(2 * M_HEADS, 1))
            h = _mlstm_core(h, k, ut, gates_t, _lane_replicated(a_h_norm[l]), a_w_out[l].T.astype(BF16))
        else:
            j = l - N_A
            qt = _q_proj(h, b_norm[j][None], b_w_q[j].T.astype(BF16), _lane_replicated(b_q_norm[j]), cos_t, sin_t)
            mixer = (_moba_attn(qt, kb, vtb, kmean, ksq), b_w_o[j].astype(BF16))
        h = _conv_ffn(h, f_norm[l][None], f_w_up[l].astype(BF16), f_conv_w[l], f_conv_b[l][None],
                      f_w_down[l].astype(BF16), *mixer)
        if l == N_A - 1:
            kb, vtb, km, ks = _kv_proj(h, kv_norm[None], w_kv.T.astype(BF16), _lane_replicated(k_norm), cos_t, sin_t)
            kmean = km.reshape(s_ // MOBA_BLOCK, D_MODEL).astype(BF16)
            ksq = ks.reshape(s_ // MOBA_BLOCK, D_MODEL)
    return h.reshape(b_, s_, d_)
```

```python
import functools

import jax
import jax.numpy as jnp
from jax import lax
from jax.experimental import pallas as pl
from jax.experimental.pallas import tpu as pltpu

F32 = jnp.float32
BF16 = jnp.bfloat16

D_MODEL = 1024
DEPTH = 4
N_A = DEPTH // 2
M_HEADS = 4
M_DQK = 128
M_DV = D_MODEL // M_HEADS
M_QKVO = 2 * M_HEADS * M_DQK + 2 * M_HEADS * M_DV
M_UT_Q = 2 * M_HEADS * M_DV
M_UT_ROWS = M_UT_Q + M_HEADS * M_DQK
A_HEADS = 8
A_DH = D_MODEL // A_HEADS
ROT_DIM = A_DH // 4
ROPE_THETA = 500000.0
MOBA_BLOCK = 256
MOBA_TOPK = 3
D_FF = 2816
CONV_W = 3
EPS = 1e-6

M_CHUNK = 256
ROW_TILE = 512
FFN_ROW_TILE = 1024
FFN_CHUNK = 256
ATTN_HEADS = 4
ATTN_UNROLL = 2
FAST_UNROLL = 8
TAIL_UNROLL = 4
LOG2_E = 1.4426950408889634
MASKED_SCORE = -1e30
MAX_EXP2_ARG = 100.0
NORM_BOUND_SLACK = 1.0 + 2.0 ** -6
VMEM_LIMIT = 56 * 1024 * 1024
NEG_INF = float("-inf")


def _rms_rows(x, gain):
    return x * lax.rsqrt(jnp.mean(x * x, axis=-1, keepdims=True) + EPS) * gain


def _nt_dot(a, b, **kw):
    return lax.dot_general(a, b, (((1,), (1,)), ((), ())), preferred_element_type=F32, **kw)


def _headnorm_rope_t(t, gain, cos, sin, scale):
    half = ROT_DIM // 2
    g = jnp.concatenate([gain] * (t.shape[1] // gain.shape[1]), axis=1)
    outs = []
    for h in range(t.shape[0] // A_DH):
        th = t[h * A_DH:(h + 1) * A_DH, :]
        th = th * lax.rsqrt(jnp.mean(th * th, axis=0, keepdims=True) + EPS) * g
        t1, t2 = th[:half], th[half:ROT_DIM]
        th = jnp.concatenate([t1 * cos - t2 * sin, t2 * cos + t1 * sin, th[ROT_DIM:]], axis=0)
        outs.append(th * scale if scale != 1.0 else th)
    return jnp.concatenate(outs, axis=0)


def _mlstm_in_kernel(x_ref, g_ref, wk_ref, wt_ref, bias_ref, tri_ref, k_ref, ut_ref, gt_ref):
    xn = _rms_rows(x_ref[...], g_ref[...])
    xh = xn.astype(BF16)
    k_ref[...] = jnp.dot(xh, wk_ref[...], preferred_element_type=F32).astype(BF16)
    ut = _nt_dot(wt_ref[...], xh)
    ut_ref[:M_UT_Q, :] = ut[:M_UT_Q].astype(BF16)
    ut_ref[M_UT_Q:, :] = (ut[M_UT_Q:M_UT_ROWS] * M_DQK ** -0.5).astype(BF16)
    xl = (xn - xh.astype(F32)).astype(BF16)
    e = ut[M_UT_ROWS:] + _nt_dot(wt_ref[M_UT_ROWS:, :], xl)
    g = e[:2 * M_HEADS] + e[2 * M_HEADS:] + bias_ref[...]
    fpre = g[M_HEADS:]
    lf = jnp.minimum(fpre, 0.0) - jnp.log1p(jnp.exp(-jnp.abs(fpre)))
    lf8 = jnp.concatenate([lf, jnp.zeros_like(lf)], axis=0)
    hi = lf8.astype(BF16).astype(F32)
    mid = (lf8 - hi).astype(BF16).astype(F32)
    lo = lf8 - hi - mid
    pieces = jnp.concatenate([hi, mid, lo, jnp.zeros_like(hi)], axis=0).astype(BF16)
    c = jnp.dot(pieces, tri_ref[...], preferred_element_type=F32)
    rows = 2 * M_HEADS
    b = c[:rows] + c[rows:2 * rows] + c[2 * rows:3 * rows]
    gt_ref[...] = jnp.concatenate([g[:M_HEADS], b[:M_HEADS]], axis=0)


def _mlstm_in_proj(x, gain, w_k, wt_voqg, gate_bias):
    s = x.shape[0]
    dk = M_HEADS * M_DQK
    pos = jnp.arange(ROW_TILE)
    tri = ((pos[:, None] <= pos[None, :]) & (pos[:, None] // M_CHUNK == pos[None, :] // M_CHUNK)).astype(BF16)
    return pl.pallas_call(
        _mlstm_in_kernel,
        grid=(s // ROW_TILE,),
        in_specs=[
            pl.BlockSpec((ROW_TILE, D_MODEL), lambda i: (i, 0)),
            pl.BlockSpec((1, D_MODEL), lambda i: (0, 0)),
            pl.BlockSpec((D_MODEL, dk), lambda i: (0, 0), pipeline_mode=pl.Buffered(1)),
            pl.BlockSpec((M_UT_ROWS + 4 * M_HEADS, D_MODEL), lambda i: (0, 0), pipeline_mode=pl.Buffered(1)),
            pl.BlockSpec((2 * M_HEADS, 1), lambda i: (0, 0)),
            pl.BlockSpec((ROW_TILE, ROW_TILE), lambda i: (0, 0)),
        ],
        out_specs=[
            pl.BlockSpec((ROW_TILE, dk), lambda i: (i, 0)),
            pl.BlockSpec((M_UT_ROWS, ROW_TILE), lambda i: (0, i)),
            pl.BlockSpec((2 * M_HEADS, ROW_TILE), lambda i: (0, i)),
        ],
        out_shape=[jax.ShapeDtypeStruct((s, dk), BF16), jax.ShapeDtypeStruct((M_UT_ROWS, s), BF16),
                   jax.ShapeDtypeStruct((2 * M_HEADS, s), F32)],
        compiler_params=pltpu.CompilerParams(
            dimension_semantics=("parallel",), vmem_limit_bytes=VMEM_LIMIT),
        name="mlstm_in_proj",
    )(x, gain, w_k, wt_voqg, gate_bias, tri)


def _mlstm_core_kernel(x_ref, k_ref, vt_ref, ot_ref, qt_ref, gt_ref, hg_ref, woutt_ref,
                       y_ref, ct_ref, n_ref, m_ref):
    L = M_CHUNK

    @pl.when(pl.program_id(0) == 0)
    def _():
        ct_ref[...] = jnp.zeros_like(ct_ref)
        n_ref[...] = jnp.zeros_like(n_ref)
        m_ref[...] = jnp.zeros_like(m_ref)

    g = gt_ref[...]
    gi, b_rows = g[:M_HEADS], g[M_HEADS:]
    s_i = lax.broadcasted_iota(jnp.int32, (L, L), 0)
    t_i = lax.broadcasted_iota(jnp.int32, (L, L), 1)
    causal = s_i <= t_i
    a_rows = gi - b_rows
    a_cols = jnp.concatenate([a_rows, jnp.zeros((128 - M_HEADS, L), F32)], axis=0).T
    hgain = jnp.concatenate([hg_ref[...]] * (L // 128), axis=1)

    heads = []
    for h in range(M_HEADS):
        qt = qt_ref[h * M_DQK:(h + 1) * M_DQK, :]
        kh = k_ref[:, h * M_DQK:(h + 1) * M_DQK]
        vt = vt_ref[h * M_DV:(h + 1) * M_DV, :]
        b_row = b_rows[h:h + 1, :]
        i_row = gi[h:h + 1, :]
        m_prev = m_ref[h:h + 1, 0:1]
        ct_prev = ct_ref[h]
        n_prev = n_ref[h]

        d = jnp.where(causal, a_cols[:, h:h + 1] + b_row, NEG_INF)
        inter = b_row + m_prev
        m_t = jnp.maximum(inter, jnp.max(d, axis=0, keepdims=True))
        w_inter = jnp.exp(inter - m_t)
        s_mat = jnp.dot(kh, qt, preferred_element_type=F32) * jnp.exp(d - m_t)
        num = w_inter * jnp.dot(ct_prev.astype(BF16), qt, preferred_element_type=F32) \
            + jnp.dot(vt, s_mat.astype(BF16), preferred_element_type=F32)
        qn = jnp.dot(n_prev.astype(BF16), qt, preferred_element_type=F32)[0:1, :]
        den = w_inter * qn + jnp.sum(s_mat, axis=0, keepdims=True)
        hh = num / jnp.maximum(jnp.abs(den), jnp.exp(-m_t))

        b_last = b_row[:, L - 1:L]
        g_row = b_last - b_row + i_row
        m_new = jnp.maximum(b_last + m_prev, jnp.max(g_row, axis=-1, keepdims=True))
        decay = jnp.exp(b_last + m_prev - m_new)
        w_row = jnp.exp(g_row - m_new)
        ct_ref[h] = decay * ct_prev + jnp.dot((vt.astype(F32) * w_row).astype(BF16), kh,
                                              preferred_element_type=F32)
        n_ref[h] = decay * n_prev + jnp.dot(jnp.broadcast_to(w_row, (8, L)).astype(BF16), kh,
                                            preferred_element_type=F32)
        m_ref[h:h + 1, :] = jnp.broadcast_to(m_new, (1, 128))

        hs = slice(h * M_DV, (h + 1) * M_DV)
        hn = hh * lax.rsqrt(jnp.mean(hh * hh, axis=0, keepdims=True) + EPS) * hgain[hs]
        heads.append((hn * jax.nn.sigmoid(ot_ref[hs, :].astype(F32))).astype(BF16))

    yt = jnp.dot(woutt_ref[...], jnp.concatenate(heads, axis=0), preferred_element_type=F32)
    y_ref[...] = x_ref[...] + yt.T


def _mlstm_core(x, k, ut, gates_t, h_gain, wt_out):
    s = x.shape[0]
    L = M_CHUNK
    dk, dv = M_HEADS * M_DQK, M_HEADS * M_DV
    return pl.pallas_call(
        _mlstm_core_kernel,
        grid=(s // L,),
        in_specs=[
            pl.BlockSpec((L, D_MODEL), lambda c: (c, 0)),
            pl.BlockSpec((L, dk), lambda c: (c, 0)),
            pl.BlockSpec((dv, L), lambda c: (0, c)),
            pl.BlockSpec((dv, L), lambda c: (1, c)),
            pl.BlockSpec((dk, L), lambda c: (M_UT_Q // dk, c)),
            pl.BlockSpec((2 * M_HEADS, L), lambda c: (0, c)),
            pl.BlockSpec((dv, 128), lambda c: (0, 0)),
            pl.BlockSpec((D_MODEL, D_MODEL), lambda c: (0, 0)),
        ],
        out_specs=pl.BlockSpec((L, D_MODEL), lambda c: (c, 0)),
        out_shape=jax.ShapeDtypeStruct((s, D_MODEL), F32),
        scratch_shapes=[
            pltpu.VMEM((M_HEADS, M_DV, M_DQK), F32),
            pltpu.VMEM((M_HEADS, 8, M_DQK), F32),
            pltpu.VMEM((8, 128), F32),
        ],
        compiler_params=pltpu.CompilerParams(
            dimension_semantics=("arbitrary",), vmem_limit_bytes=VMEM_LIMIT),
        name="mlstm_core",
    )(x, k, ut, ut, ut, gates_t, h_gain, wt_out)


def _conv_ffn_kernel(*refs, mixer_proj):
    if mixer_proj:
        x_ref, a_ref, wo_ref, g_ref, wup_ref, cw_ref, cb_ref, wdown_ref, y_ref, carry_ref, act_ref = refs
    else:
        x_ref, g_ref, wup_ref, cw_ref, cb_ref, wdown_ref, y_ref, carry_ref, act_ref = refs
    tm = x_ref.shape[0]

    @pl.when(pl.program_id(0) == 0)
    def _():
        carry_ref[...] = jnp.zeros_like(carry_ref)

    x = x_ref[...]
    if mixer_proj:
        x = x + jnp.dot(a_ref[...], wo_ref[...], preferred_element_type=F32)
    xn = _rms_rows(x, g_ref[...]).astype(BF16)
    top = lax.broadcasted_iota(jnp.int32, (8, FFN_CHUNK), 0)

    def conv_cols(col):
        u = jnp.dot(xn, wup_ref[:, col:col + FFN_CHUNK], preferred_element_type=F32)
        prev = carry_ref[:, col:col + FFN_CHUNK]
        u1 = pltpu.roll(u, 1, 0)
        u2 = pltpu.roll(u, 2, 0)
        u1_top = jnp.where(top == 0, prev[7:8], u1[0:8])
        u2_top = jnp.where(top == 0, prev[6:7], jnp.where(top == 1, prev[7:8], u2[0:8]))
        u1 = jnp.concatenate([u1_top, u1[8:]], axis=0)
        u2 = jnp.concatenate([u2_top, u2[8:]], axis=0)
        carry_ref[:, col:col + FFN_CHUNK] = u[tm - 8:tm]
        cw = cw_ref[:, col:col + FFN_CHUNK]
        return cb_ref[:, col:col + FFN_CHUNK] + u2 * cw[0:1] + u1 * cw[1:2] + u * cw[2:3]

    for c in range(D_FF // FFN_CHUNK):
        val = conv_cols(c * FFN_CHUNK)
        gate = conv_cols(D_FF + c * FFN_CHUNK)
        act_ref[:, c * FFN_CHUNK:(c + 1) * FFN_CHUNK] = (gate * jax.nn.sigmoid(gate) * val).astype(BF16)

    y_ref[...] = x + jnp.dot(act_ref[...], wdown_ref[...], preferred_element_type=F32)


def _conv_ffn(x, gain, w_up, conv_w, conv_b, w_down, mixer_out=None, w_o=None):
    s = x.shape[0]
    const = lambda i: (0, 0)
    row_spec = pl.BlockSpec((FFN_ROW_TILE, D_MODEL), lambda i: (i, 0))
    mixer_proj = mixer_out is not None
    mixer_specs = [row_spec, pl.BlockSpec((D_MODEL, D_MODEL), const, pipeline_mode=pl.Buffered(1))]
    return pl.pallas_call(
        functools.partial(_conv_ffn_kernel, mixer_proj=mixer_proj),
        grid=(s // FFN_ROW_TILE,),
        in_specs=[row_spec] + (mixer_specs if mixer_proj else []) + [
            pl.BlockSpec((1, D_MODEL), const),
            pl.BlockSpec((D_MODEL, 2 * D_FF), const, pipeline_mode=pl.Buffered(1)),
            pl.BlockSpec((CONV_W, 2 * D_FF), const),
            pl.BlockSpec((1, 2 * D_FF), const),
            pl.BlockSpec((D_FF, D_MODEL), const, pipeline_mode=pl.Buffered(1)),
        ],
        out_specs=pl.BlockSpec((FFN_ROW_TILE, D_MODEL), lambda i: (i, 0)),
        out_shape=jax.ShapeDtypeStruct((s, D_MODEL), F32),
        scratch_shapes=[pltpu.VMEM((8, 2 * D_FF), F32), pltpu.VMEM((FFN_ROW_TILE, D_FF), BF16)],
        compiler_params=pltpu.CompilerParams(
            dimension_semantics=("arbitrary",), vmem_limit_bytes=VMEM_LIMIT),
        name="conv_ffn",
    )(x, *((mixer_out, w_o) if mixer_proj else ()), gain, w_up, conv_w, conv_b, w_down)


def _kv_proj_kernel(x_ref, g_ref, wt_ref, kg_ref, cos_ref, sin_ref, k_ref, vt_ref, km_ref, ksq_ref):
    nb = k_ref.shape[0]
    xn = _rms_rows(x_ref[...], g_ref[...]).astype(BF16)
    kt = _headnorm_rope_t(_nt_dot(wt_ref[:D_MODEL, :], xn), kg_ref[...], cos_ref[...], sin_ref[...], 1.0)
    kbt = kt.astype(BF16)
    k = kt.T.reshape(nb, MOBA_BLOCK, D_MODEL)
    km_ref[...] = jnp.mean(k, axis=1, keepdims=True)
    k_ref[...] = k.astype(BF16)
    kf = kbt.astype(F32)
    rows = []
    for b in range(nb):
        per_head = []
        for h in range(A_HEADS):
            kh = kf[h * A_DH:(h + 1) * A_DH, b * MOBA_BLOCK:(b + 1) * MOBA_BLOCK]
            n2 = jnp.max(jnp.sum(kh * kh, axis=0, keepdims=True), axis=1, keepdims=True)
            per_head.append(jnp.broadcast_to(n2, (1, A_DH)))
        rows.append(jnp.concatenate(per_head, axis=1)[None])
    ksq_ref[...] = jnp.concatenate(rows, axis=0)

    vt = _nt_dot(wt_ref[D_MODEL:, :], xn).astype(BF16)
    for b in range(nb):
        vt_ref[b] = vt[:, b * MOBA_BLOCK:(b + 1) * MOBA_BLOCK]


def _kv_proj(x, gain, wt_kv, k_gain, cos_t, sin_t):
    s = x.shape[0]
    nb = ROW_TILE // MOBA_BLOCK
    n_blocks = s // MOBA_BLOCK
    half = ROT_DIM // 2
    return pl.pallas_call(
        _kv_proj_kernel,
        grid=(s // ROW_TILE,),
        in_specs=[
            pl.BlockSpec((ROW_TILE, D_MODEL), lambda i: (i, 0)),
            pl.BlockSpec((1, D_MODEL), lambda i: (0, 0)),
            pl.BlockSpec((2 * D_MODEL, D_MODEL), lambda i: (0, 0), pipeline_mode=pl.Buffered(1)),
            pl.BlockSpec((A_DH, A_DH), lambda i: (0, 0)),
            pl.BlockSpec((half, ROW_TILE), lambda i: (0, i)),
            pl.BlockSpec((half, ROW_TILE), lambda i: (0, i)),
        ],
        out_specs=[
            pl.BlockSpec((nb, MOBA_BLOCK, D_MODEL), lambda i: (i, 0, 0)),
            pl.BlockSpec((nb, D_MODEL, MOBA_BLOCK), lambda i: (i, 0, 0)),
            pl.BlockSpec((nb, 1, D_MODEL), lambda i: (i, 0, 0)),
            pl.BlockSpec((nb, 1, D_MODEL), lambda i: (i, 0, 0)),
        ],
        out_shape=[
            jax.ShapeDtypeStruct((n_blocks, MOBA_BLOCK, D_MODEL), BF16),
            jax.ShapeDtypeStruct((n_blocks, D_MODEL, MOBA_BLOCK), BF16),
            jax.ShapeDtypeStruct((n_blocks, 1, D_MODEL), F32),
            jax.ShapeDtypeStruct((n_blocks, 1, D_MODEL), F32),
        ],
        compiler_params=pltpu.CompilerParams(
            dimension_semantics=("parallel",), vmem_limit_bytes=VMEM_LIMIT),
        name="kv_proj",
    )(x, gain, wt_kv, k_gain, cos_t, sin_t)


def _q_proj_kernel(x_ref, g_ref, wt_ref, qg_ref, cos_ref, sin_ref, qt_ref):
    xn = _rms_rows(x_ref[...], g_ref[...]).astype(BF16)
    qt = _headnorm_rope_t(_nt_dot(wt_ref[...], xn), qg_ref[...], cos_ref[...], sin_ref[...], A_DH ** -0.5 * LOG2_E)
    qt_ref[...] = qt.astype(BF16)


def _q_proj(x, gain, wt_q, q_gain, cos_t, sin_t):
    s = x.shape[0]
    half = ROT_DIM // 2
    return pl.pallas_call(
        _q_proj_kernel,
        grid=(s // ROW_TILE,),
        in_specs=[
            pl.BlockSpec((ROW_TILE, D_MODEL), lambda i: (i, 0)),
            pl.BlockSpec((1, D_MODEL), lambda i: (0, 0)),
            pl.BlockSpec((D_MODEL, D_MODEL), lambda i: (0, 0)),
            pl.BlockSpec((A_DH, A_DH), lambda i: (0, 0)),
            pl.BlockSpec((half, ROW_TILE), lambda i: (0, i)),
            pl.BlockSpec((half, ROW_TILE), lambda i: (0, i)),
        ],
        out_specs=pl.BlockSpec((D_MODEL, ROW_TILE), lambda i: (0, i)),
        out_shape=jax.ShapeDtypeStruct((D_MODEL, s), BF16),
        compiler_params=pltpu.CompilerParams(
            dimension_semantics=("parallel",), vmem_limit_bytes=VMEM_LIMIT),
        name="q_proj",
    )(x, gain, wt_q, q_gain, cos_t, sin_t)


def _moba_attn_kernel(qt_ref, k_ref, vt_ref, km_ref, ksq_ref, o_ref, sel_ref, bias_ref, p_ref):
    nb = k_ref.shape[0]
    bs = MOBA_BLOCK
    cur = pl.program_id(1)
    kpos = lax.broadcasted_iota(jnp.int32, (bs, bs), 0)
    qpos = lax.broadcasted_iota(jnp.int32, (bs, bs), 1)
    causal = kpos <= qpos

    qts, own, gaps = [], [], []
    for hh in range(ATTN_HEADS):
        hs = slice(hh * A_DH, (hh + 1) * A_DH)
        qt = qt_ref[hs, :]
        qts.append(qt)
        gate = jnp.dot(km_ref[:, hs], qt, preferred_element_type=F32)
        blk = lax.broadcasted_iota(jnp.int32, gate.shape, 0)
        gate = jnp.where(blk < cur, gate, NEG_INF)
        sel = jnp.zeros(gate.shape, F32)
        for _ in range(MOBA_TOPK):
            top = jnp.max(gate, axis=0, keepdims=True)
            first = jnp.min(jnp.where(gate == top, blk, nb), axis=0, keepdims=True)
            first = jnp.where(top > NEG_INF, first, nb)
            pick = blk == first
            sel = jnp.where(pick, 1.0, sel)
            gate = jnp.where(pick, NEG_INF, gate)
        sel_ref[hh] = sel

        s_own = jnp.dot(k_ref[cur, :, hs], qt, preferred_element_type=F32)
        s_own = jnp.where(causal, s_own, NEG_INF)
        m0 = jnp.max(s_own, axis=0, keepdims=True)
        own.append((m0, jnp.exp2(s_own - m0)))
        bias_ref[hh] = jnp.where(sel > 0.0, -m0, MASKED_SCORE)

        qf = qt.astype(F32)
        q2 = jnp.sum(qf * qf, axis=0, keepdims=True)
        k2 = jnp.max(ksq_ref[:, hs], axis=0, keepdims=True)[:, 0:1]
        gaps.append(jnp.sqrt(q2 * k2) * NORM_BOUND_SLACK - m0)

    def finish(parts):
        o_ref[...] = jnp.concatenate([(acc / l).T for acc, l in parts], axis=1).astype(BF16)

    def fixed_max_path():
        def sublane_partial(p):
            return jnp.sum(p.reshape(p.shape[0] // 8, 8, p.shape[1]), axis=0)

        init = []
        for hh in range(ATTN_HEADS):
            hs = slice(hh * A_DH, (hh + 1) * A_DH)
            p0 = own[hh][1]
            init.append((jnp.dot(vt_ref[cur, hs, :], p0.astype(BF16), preferred_element_type=F32),
                         sublane_partial(p0)))

        def make_body(unroll, base):
            def body(step, carry):
                j0 = base + step * unroll
                out = []
                for hh in range(ATTN_HEADS):
                    hs = slice(hh * A_DH, (hh + 1) * A_DH)
                    acc, lsum = carry[hh]
                    p_parts, vt_parts = [], []
                    for b in range(unroll):
                        s_b = jnp.dot(k_ref[j0 + b, :, hs], qts[hh], preferred_element_type=F32)
                        p_b = jnp.exp2(s_b + bias_ref[hh, pl.ds(j0 + b, 1), :])
                        lsum = lsum + sublane_partial(p_b)
                        p_parts.append(p_b.astype(BF16))
                        vt_parts.append(vt_ref[j0 + b, hs, :])
                    acc = acc + jnp.dot(jnp.concatenate(vt_parts, axis=1), jnp.concatenate(p_parts, axis=0),
                                        preferred_element_type=F32)
                    out.append((acc, lsum))
                return tuple(out)
            return body

        def probs(hh, j0, lsum):
            hs = slice(hh * A_DH, (hh + 1) * A_DH)
            for b in range(FAST_UNROLL):
                s_b = jnp.dot(k_ref[j0 + b, :, hs], qts[hh], preferred_element_type=F32)
                p_b = jnp.exp2(s_b + bias_ref[hh, pl.ds(j0 + b, 1), :])
                lsum = lsum + sublane_partial(p_b)
                p_ref[hh, b * bs:(b + 1) * bs, :] = p_b.astype(BF16)
            return lsum

        def weighted_values(hh, j0, acc):
            hs = slice(hh * A_DH, (hh + 1) * A_DH)
            vt = jnp.concatenate([vt_ref[j0 + b, hs, :] for b in range(FAST_UNROLL)], axis=1)
            return acc + jnp.dot(vt, p_ref[hh], preferred_element_type=F32)

        n_main = jnp.maximum(cur // FAST_UNROLL, 1)
        carry = tuple((acc, probs(hh, 0, lsum)) for hh, (acc, lsum) in enumerate(init))

        def main_body(step, carry):
            accs = [weighted_values(hh, (step - 1) * FAST_UNROLL, acc) for hh, (acc, _) in enumerate(carry)]
            return tuple((accs[hh], probs(hh, step * FAST_UNROLL, lsum)) for hh, (_, lsum) in enumerate(carry))

        carry = lax.fori_loop(1, n_main, main_body, carry)
        carry = tuple((weighted_values(hh, (n_main - 1) * FAST_UNROLL, acc), lsum)
                      for hh, (acc, lsum) in enumerate(carry))
        done = n_main * FAST_UNROLL
        carry = lax.fori_loop(0, (cur - done + TAIL_UNROLL - 1) // TAIL_UNROLL, make_body(TAIL_UNROLL, done), carry)
        finish([(acc, jnp.sum(lsum, axis=0, keepdims=True)) for acc, lsum in carry])

    def running_max_path():
        init = []
        for hh in range(ATTN_HEADS):
            hs = slice(hh * A_DH, (hh + 1) * A_DH)
            m0, p0 = own[hh]
            init.append((m0, jnp.sum(p0, axis=0, keepdims=True),
                         jnp.dot(vt_ref[cur, hs, :], p0.astype(BF16), preferred_element_type=F32)))

        def body(step, carry):
            j0 = step * ATTN_UNROLL
            out = []
            for hh in range(ATTN_HEADS):
                hs = slice(hh * A_DH, (hh + 1) * A_DH)
                m, l, acc = carry[hh]
                s_parts, vt_parts = [], []
                for b in range(ATTN_UNROLL):
                    s_b = jnp.dot(k_ref[j0 + b, :, hs], qts[hh], preferred_element_type=F32)
                    s_parts.append(jnp.where(sel_ref[hh, pl.ds(j0 + b, 1), :] > 0.0, s_b, NEG_INF))
                    vt_parts.append(vt_ref[j0 + b, hs, :])
                s_t = jnp.concatenate(s_parts, axis=0)
                m_new = jnp.maximum(m, jnp.max(s_t, axis=0, keepdims=True))
                alpha = jnp.exp2(m - m_new)
                p = jnp.exp2(s_t - m_new)
                l = alpha * l + jnp.sum(p, axis=0, keepdims=True)
                acc = alpha * acc + jnp.dot(jnp.concatenate(vt_parts, axis=1), p.astype(BF16),
                                            preferred_element_type=F32)
                out.append((m_new, l, acc))
            return tuple(out)

        final = lax.fori_loop(0, (cur + ATTN_UNROLL - 1) // ATTN_UNROLL, body, tuple(init))
        finish([(acc, l) for _, l, acc in final])

    worst_gap = jnp.max(functools.reduce(jnp.maximum, gaps))
    lax.cond(worst_gap <= MAX_EXP2_ARG, fixed_max_path, running_max_path)


def _moba_attn(qt, kb, vtb, kmean, ksq):
    s = qt.shape[1]
    nb = s // MOBA_BLOCK
    gw = ATTN_HEADS * A_DH
    return pl.pallas_call(
        _moba_attn_kernel,
        grid=(A_HEADS // ATTN_HEADS, nb),
        in_specs=[
            pl.BlockSpec((gw, MOBA_BLOCK), lambda g, i: (g, i)),
            pl.BlockSpec((nb, MOBA_BLOCK, gw), lambda g, i: (0, 0, g), pipeline_mode=pl.Buffered(1)),
            pl.BlockSpec((nb, gw, MOBA_BLOCK), lambda g, i: (0, g, 0), pipeline_mode=pl.Buffered(1)),
            pl.BlockSpec((nb, gw), lambda g, i: (0, g)),
            pl.BlockSpec((nb, gw), lambda g, i: (0, g)),
        ],
        out_specs=pl.BlockSpec((MOBA_BLOCK, gw), lambda g, i: (i, g)),
        out_shape=jax.ShapeDtypeStruct((s, D_MODEL), BF16),
        scratch_shapes=[pltpu.VMEM((ATTN_HEADS, nb, MOBA_BLOCK), F32),
                        pltpu.VMEM((ATTN_HEADS, nb, MOBA_BLOCK), F32),
                        pltpu.VMEM((ATTN_HEADS, FAST_UNROLL * MOBA_BLOCK, MOBA_BLOCK), BF16)],
        compiler_params=pltpu.CompilerParams(
            dimension_semantics=("parallel", "arbitrary"), vmem_limit_bytes=VMEM_LIMIT),
        name="moba_attn",
    )(qt, kb, vtb, kmean, ksq)


def _rope_tables_t(s_len):
    pos = jnp.arange(s_len, dtype=F32)
    inv = ROPE_THETA ** (-jnp.arange(0, ROT_DIM, 2, dtype=F32) / ROT_DIM)
    ang = inv[:, None] * pos[None, :]
    return jnp.cos(ang), jnp.sin(ang)


def _lane_replicated(gain):
    return jnp.broadcast_to(gain[:, None], (gain.shape[0], 128))


def kernel(x, a_norm, a_w_in, a_b_gates, a_h_norm, a_w_out, kv_norm, w_kv, k_norm, b_norm, b_w_q, b_q_norm,
           b_w_o, f_norm, f_w_up, f_conv_w, f_conv_b, f_w_down):
    b_, s_, d_ = x.shape
    assert (b_, d_) == (1, D_MODEL) and s_ % ROW_TILE == 0 and s_ % MOBA_BLOCK == 0
    cos_t, sin_t = _rope_tables_t(s_)
    h = x.reshape(s_, d_)
    kb = vtb = kmean = ksq = None
    for l in range(DEPTH):
        mixer = ()
        if l < N_A:
            w_in = a_w_in[l]
            wg = w_in[:, M_QKVO:]
            wg_hi = wg.astype(BF16)
            wg_lo = (wg - wg_hi.astype(F32)).astype(BF16)
            dk = M_HEADS * M_DQK
            wt_voqg = jnp.concatenate([w_in[:, 2 * dk:M_QKVO].astype(BF16), w_in[:, :dk].astype(BF16),
                                       wg_hi, wg_lo], axis=1).T
            k, ut, gates_t = _mlstm_in_proj(h, a_norm[l][None], w_in[:, dk:2 * dk].astype(BF16), wt_voqg,
                                            a_b_gates[l].reshape(2 * M_HEADS, 1))
            h = _mlstm_core(h, k, ut, gates_t, _lane_replicated(a_h_norm[l]), a_w_out[l].T.astype(BF16))
        else:
            j = l - N_A
            qt = _q_proj(h, b_norm[j][None], b_w_q[j].T.astype(BF16), _lane_replicated(b_q_norm[j]), cos_t, sin_t)
            mixer = (_moba_attn(qt, kb, vtb, kmean, ksq), b_w_o[j].astype(BF16))
        h = _conv_ffn(h, f_norm[l][None], f_w_up[l].astype(BF16), f_conv_w[l], f_conv_b[l][None],
                      f_w_down[l].astype(BF16), *mixer)
        if l == N_A - 1:
            kb, vtb, km, ks = _kv_proj(h, kv_norm[None], w_kv.T.astype(BF16), _lane_replicated(k_norm), cos_t, sin_t)
            kmean = km.reshape(s_ // MOBA_BLOCK, D_MODEL).astype(BF16)
            ksq = ks.reshape(s_ // MOBA_BLOCK, D_MODEL)
    return h.reshape(b_, s_, d_)
```

```python
import functools

import jax
import jax.numpy as jnp
from jax import lax
from jax.experimental import pallas as pl
from jax.experimental.pallas import tpu as pltpu

F32 = jnp.float32
BF16 = jnp.bfloat16

D_MODEL = 1024
DEPTH = 4
N_A = DEPTH // 2
M_HEADS = 4
M_DQK = 128
M_DV = D_MODEL // M_HEADS
M_QKVO = 2 * M_HEADS * M_DQK + 2 * M_HEADS * M_DV
M_UT_Q = 2 * M_HEADS * M_DV
M_UT_ROWS = M_UT_Q + M_HEADS * M_DQK
A_HEADS = 8
A_DH = D_MODEL // A_HEADS
ROT_DIM = A_DH // 4
ROPE_THETA = 500000.0
MOBA_BLOCK = 256
MOBA_TOPK = 3
D_FF = 2816
CONV_W = 3
EPS = 1e-6

M_CHUNK = 256
ROW_TILE = 512
FFN_ROW_TILE = 1024
FFN_CHUNK = 256
ATTN_HEADS = 4
ATTN_UNROLL = 2
FAST_UNROLL = 8
TAIL_UNROLL = 4
LOG2_E = 1.4426950408889634
MASKED_SCORE = -1e30
MAX_EXP2_ARG = 100.0
NORM_BOUND_SLACK = 1.0 + 2.0 ** -6
VMEM_LIMIT = 56 * 1024 * 1024
NEG_INF = float("-inf")


def _rms_rows(x, gain):
    return x * lax.rsqrt(jnp.mean(x * x, axis=-1, keepdims=True) + EPS) * gain


def _nt_dot(a, b, **kw):
    return lax.dot_general(a, b, (((1,), (1,)), ((), ())), preferred_element_type=F32, **kw)


def _headnorm_rope_t(t, gain, cos, sin, scale):
    half = ROT_DIM // 2
    g = jnp.concatenate([gain] * (t.shape[1] // gain.shape[1]), axis=1)
    outs = []
    for h in range(t.shape[0] // A_DH):
        th = t[h * A_DH:(h + 1) * A_DH, :]
        th = th * lax.rsqrt(jnp.mean(th * th, axis=0, keepdims=True) + EPS) * g
        t1, t2 = th[:half], th[half:ROT_DIM]
        th = jnp.concatenate([t1 * cos - t2 * sin, t2 * cos + t1 * sin, th[ROT_DIM:]], axis=0)
        outs.append(th * scale if scale != 1.0 else th)
    return jnp.concatenate(outs, axis=0)


def _mlstm_in_kernel(x_ref, g_ref, wk_ref, wt_ref, bias_ref, tri_ref, k_ref, ut_ref, gt_ref):
    xn = _rms_rows(x_ref[...], g_ref[...])
    xh = xn.astype(BF16)
    k_ref[...] = jnp.dot(xh, wk_ref[...], preferred_element_type=F32).astype(BF16)
    ut = _nt_dot(wt_ref[...], xh)
    ut_ref[:M_UT_Q, :] = ut[:M_UT_Q].astype(BF16)
    ut_ref[M_UT_Q:, :] = (ut[M_UT_Q:M_UT_ROWS] * M_DQK ** -0.5).astype(BF16)
    xl = (xn - xh.astype(F32)).astype(BF16)
    e = ut[M_UT_ROWS:] + _nt_dot(wt_ref[M_UT_ROWS:, :], xl)
    g = e[:2 * M_HEADS] + e[2 * M_HEADS:] + bias_ref[...]
    fpre = g[M_HEADS:]
    lf = jnp.minimum(fpre, 0.0) - jnp.log1p(jnp.exp(-jnp.abs(fpre)))
    lf8 = jnp.concatenate([lf, jnp.zeros_like(lf)], axis=0)
    hi = lf8.astype(BF16).astype(F32)
    mid = (lf8 - hi).astype(BF16).astype(F32)
    lo = lf8 - hi - mid
    pieces = jnp.concatenate([hi, mid, lo, jnp.zeros_like(hi)], axis=0).astype(BF16)
    c = jnp.dot(pieces, tri_ref[...], preferred_element_type=F32)
    rows = 2 * M_HEADS
    b = c[:rows] + c[rows:2 * rows] + c[2 * rows:3 * rows]
    gt_ref[...] = jnp.concatenate([g[:M_HEADS], b[:M_HEADS]], axis=0)


def _mlstm_in_proj(x, gain, w_k, wt_voqg, gate_bias):
    s = x.shape[0]
    dk = M_HEADS * M_DQK
    pos = jnp.arange(ROW_TILE)
    tri = ((pos[:, None] <= pos[None, :]) & (pos[:, None] // M_CHUNK == pos[None, :] // M_CHUNK)).astype(BF16)
    return pl.pallas_call(
        _mlstm_in_kernel,
        grid=(s // ROW_TILE,),
        in_specs=[
            pl.BlockSpec((ROW_TILE, D_MODEL), lambda i: (i, 0)),
            pl.BlockSpec((1, D_MODEL), lambda i: (0, 0)),
            pl.BlockSpec((D_MODEL, dk), lambda i: (0, 0), pipeline_mode=pl.Buffered(1)),
            pl.BlockSpec((M_UT_ROWS + 4 * M_HEADS, D_MODEL), lambda i: (0, 0), pipeline_mode=pl.Buffered(1)),
            pl.BlockSpec((2 * M_HEADS, 1), lambda i: (0, 0)),
            pl.BlockSpec((ROW_TILE, ROW_TILE), lambda i: (0, 0)),
        ],
        out_specs=[
            pl.BlockSpec((ROW_TILE, dk), lambda i: (i, 0)),
            pl.BlockSpec((M_UT_ROWS, ROW_TILE), lambda i: (0, i)),
            pl.BlockSpec((2 * M_HEADS, ROW_TILE), lambda i: (0, i)),
        ],
        out_shape=[jax.ShapeDtypeStruct((s, dk), BF16), jax.ShapeDtypeStruct((M_UT_ROWS, s), BF16),
                   jax.ShapeDtypeStruct((2 * M_HEADS, s), F32)],
        compiler_params=pltpu.CompilerParams(
            dimension_semantics=("parallel",), vmem_limit_bytes=VMEM_LIMIT),
        name="mlstm_in_proj",
    )(x, gain, w_k, wt_voqg, gate_bias, tri)


def _mlstm_core_kernel(x_ref, k_ref, vt_ref, ot_ref, qt_ref, gt_ref, hg_ref, woutt_ref,
                       y_ref, ct_ref, n_ref, m_ref):
    L = M_CHUNK

    @pl.when(pl.program_id(0) == 0)
    def _():
        ct_ref[...] = jnp.zeros_like(ct_ref)
        n_ref[...] = jnp.zeros_like(n_ref)
        m_ref[...] = jnp.zeros_like(m_ref)

    g = gt_ref[...]
    gi, b_rows = g[:M_HEADS], g[M_HEADS:]
    s_i = lax.broadcasted_iota(jnp.int32, (L, L), 0)
    t_i = lax.broadcasted_iota(jnp.int32, (L, L), 1)
    causal = s_i <= t_i
    a_rows = gi - b_rows
    a_cols = jnp.concatenate([a_rows, jnp.zeros((128 - M_HEADS, L), F32)], axis=0).T
    hgain = jnp.concatenate([hg_ref[...]] * (L // 128), axis=1)

    heads = []
    for h in range(M_HEADS):
        qt = qt_ref[h * M_DQK:(h + 1) * M_DQK, :]
        kh = k_ref[:, h * M_DQK:(h + 1) * M_DQK]
        vt = vt_ref[h * M_DV:(h + 1) * M_DV, :]
        b_row = b_rows[h:h + 1, :]
        i_row = gi[h:h + 1, :]
        m_prev = m_ref[h:h + 1, 0:1]
        ct_prev = ct_ref[h]
        n_prev = n_ref[h]

        d = jnp.where(causal, a_cols[:, h:h + 1] + b_row, NEG_INF)
        inter = b_row + m_prev
        m_t = jnp.maximum(inter, jnp.max(d, axis=0, keepdims=True))
        w_inter = jnp.exp(inter - m_t)
        s_mat = jnp.dot(kh, qt, preferred_element_type=F32) * jnp.exp(d - m_t)
        num = w_inter * jnp.dot(ct_prev.astype(BF16), qt, preferred_element_type=F32) \
            + jnp.dot(vt, s_mat.astype(BF16), preferred_element_type=F32)
        qn = jnp.dot(n_prev.astype(BF16), qt, preferred_element_type=F32)[0:1, :]
        den = w_inter * qn + jnp.sum(s_mat, axis=0, keepdims=True)
        hh = num / jnp.maximum(jnp.abs(den), jnp.exp(-m_t))

        b_last = b_row[:, L - 1:L]
        g_row = b_last - b_row + i_row
        m_new = jnp.maximum(b_last + m_prev, jnp.max(g_row, axis=-1, keepdims=True))
        decay = jnp.exp(b_last + m_prev - m_new)
        w_row = jnp.exp(g_row - m_new)
        ct_ref[h] = decay * ct_prev + jnp.dot((vt.astype(F32) * w_row).astype(BF16), kh,
                                              preferred_element_type=F32)
        n_ref[h] = decay * n_prev + jnp.dot(jnp.broadcast_to(w_row, (8, L)).astype(BF16), kh,
                                            preferred_element_type=F32)
        m_ref[h:h + 1, :] = jnp.broadcast_to(m_new, (1, 128))

        hs = slice(h * M_DV, (h + 1) * M_DV)
        hn = hh * lax.rsqrt(jnp.mean(hh * hh, axis=0, keepdims=True) + EPS) * hgain[hs]
        heads.append((hn * jax.nn.sigmoid(ot_ref[hs, :].astype(F32))).astype(BF16))

    yt = jnp.dot(woutt_ref[...], jnp.concatenate(heads, axis=0), preferred_element_type=F32)
    y_ref[...] = x_ref[...] + yt.T


def _mlstm_core(x, k, ut, gates_t, h_gain, wt_out):
    s = x.shape[0]
    L = M_CHUNK
    dk, dv = M_HEADS * M_DQK, M_HEADS * M_DV
    return pl.pallas_call(
        _mlstm_core_kernel,
        grid=(s // L,),
        in_specs=[
            pl.BlockSpec((L, D_MODEL), lambda c: (c, 0)),
            pl.BlockSpec((L, dk), lambda c: (c, 0)),
            pl.BlockSpec((dv, L), lambda c: (0, c)),
            pl.BlockSpec((dv, L), lambda c: (1, c)),
            pl.BlockSpec((dk, L), lambda c: (M_UT_Q // dk, c)),
            pl.BlockSpec((2 * M_HEADS, L), lambda c: (0, c)),
            pl.BlockSpec((dv, 128), lambda c: (0, 0)),
            pl.BlockSpec((D_MODEL, D_MODEL), lambda c: (0, 0)),
        ],
        out_specs=pl.BlockSpec((L, D_MODEL), lambda c: (c, 0)),
        out_shape=jax.ShapeDtypeStruct((s, D_MODEL), F32),
        scratch_shapes=[
            pltpu.VMEM((M_HEADS, M_DV, M_DQK), F32),
            pltpu.VMEM((M_HEADS, 8, M_DQK), F32),
            pltpu.VMEM((8, 128), F32),
        ],
        compiler_params=pltpu.CompilerParams(
            dimension_semantics=("arbitrary",), vmem_limit_bytes=VMEM_LIMIT),
        name="mlstm_core",
    )(x, k, ut, ut, ut, gates_t, h_gain, wt_out)


def _conv_ffn_kernel(*refs, mixer_proj):
    if mixer_proj:
        x_ref, a_ref, wo_ref, g_ref, wup_ref, cw_ref, cb_ref, wdown_ref, y_ref, carry_ref, act_ref = refs
    else:
        x_ref, g_ref, wup_ref, cw_ref, cb_ref, wdown_ref, y_ref, carry_ref, act_ref = refs
    tm = x_ref.shape[0]

    @pl.when(pl.program_id(0) == 0)
    def _():
        carry_ref[...] = jnp.zeros_like(carry_ref)

    x = x_ref[...]
    if mixer_proj:
        x = x + jnp.dot(a_ref[...], wo_ref[...], preferred_element_type=F32)
    xn = _rms_rows(x, g_ref[...]).astype(BF16)
    top = lax.broadcasted_iota(jnp.int32, (8, FFN_CHUNK), 0)

    def conv_cols(col):
        u = jnp.dot(xn, wup_ref[:, col:col + FFN_CHUNK], preferred_element_type=F32)
        prev = carry_ref[:, col:col + FFN_CHUNK]
        u1 = pltpu.roll(u, 1, 0)
        u2 = pltpu.roll(u, 2, 0)
        u1_top = jnp.where(top == 0, prev[7:8], u1[0:8])
        u2_top = jnp.where(top == 0, prev[6:7], jnp.where(top == 1, prev[7:8], u2[0:8]))
        u1 = jnp.concatenate([u1_top, u1[8:]], axis=0)
        u2 = jnp.concatenate([u2_top, u2[8:]], axis=0)
        carry_ref[:, col:col + FFN_CHUNK] = u[tm - 8:tm]
        cw = cw_ref[:, col:col + FFN_CHUNK]
        return cb_ref[:, col:col + FFN_CHUNK] + u2 * cw[0:1] + u1 * cw[1:2] + u * cw[2:3]

    for c in range(D_FF // FFN_CHUNK):
        val = conv_cols(c * FFN_CHUNK)
        gate = conv_cols(D_FF + c * FFN_CHUNK)
        act_ref[:, c * FFN_CHUNK:(c + 1) * FFN_CHUNK] = (gate * jax.nn.sigmoid(gate) * val).astype(BF16)

    y_ref[...] = x + jnp.dot(act_ref[...], wdown_ref[...], preferred_element_type=F32)


def _conv_ffn(x, gain, w_up, conv_w, conv_b, w_down, mixer_out=None, w_o=None):
    s = x.shape[0]
    const = lambda i: (0, 0)
    row_spec = pl.BlockSpec((FFN_ROW_TILE, D_MODEL), lambda i: (i, 0))
    mixer_proj = mixer_out is not None
    mixer_specs = [row_spec, pl.BlockSpec((D_MODEL, D_MODEL), const, pipeline_mode=pl.Buffered(1))]
    return pl.pallas_call(
        functools.partial(_conv_ffn_kernel, mixer_proj=mixer_proj),
        grid=(s // FFN_ROW_TILE,),
        in_specs=[row_spec] + (mixer_specs if mixer_proj else []) + [
            pl.BlockSpec((1, D_MODEL), const),
            pl.BlockSpec((D_MODEL, 2 * D_FF), const, pipeline_mode=pl.Buffered(1)),
            pl.BlockSpec((CONV_W, 2 * D_FF), const),
            pl.BlockSpec((1, 2 * D_FF), const),
            pl.BlockSpec((D_FF, D_MODEL), const, pipeline_mode=pl.Buffered(1)),
        ],
        out_specs=pl.BlockSpec((FFN_ROW_TILE, D_MODEL), lambda i: (i, 0)),
        out_shape=jax.ShapeDtypeStruct((s, D_MODEL), F32),
        scratch_shapes=[pltpu.VMEM((8, 2 * D_FF), F32), pltpu.VMEM((FFN_ROW_TILE, D_FF), BF16)],
        compiler_params=pltpu.CompilerParams(
            dimension_semantics=("arbitrary",), vmem_limit_bytes=VMEM_LIMIT),
        name="conv_ffn",
    )(x, *((mixer_out, w_o) if mixer_proj else ()), gain, w_up, conv_w, conv_b, w_down)


def _kv_proj_kernel(x_ref, g_ref, wt_ref, kg_ref, cos_ref, sin_ref, k_ref, vt_ref, km_ref, ksq_ref):
    nb = k_ref.shape[0]
    xn = _rms_rows(x_ref[...], g_ref[...]).astype(BF16)
    kt = _headnorm_rope_t(_nt_dot(wt_ref[:D_MODEL, :], xn), kg_ref[...], cos_ref[...], sin_ref[...], 1.0)
    kbt = kt.astype(BF16)
    k = kt.T.reshape(nb, MOBA_BLOCK, D_MODEL)
    km_ref[...] = jnp.mean(k, axis=1, keepdims=True)
    k_ref[...] = k.astype(BF16)
    kf = kbt.astype(F32)
    rows = []
    for b in range(nb):
        per_head = []
        for h in range(A_HEADS):
            kh = kf[h * A_DH:(h + 1) * A_DH, b * MOBA_BLOCK:(b + 1) * MOBA_BLOCK]
            n2 = jnp.max(jnp.sum(kh * kh, axis=0, keepdims=True), axis=1, keepdims=True)
            per_head.append(jnp.broadcast_to(n2, (1, A_DH)))
        rows.append(jnp.concatenate(per_head, axis=1)[None])
    ksq_ref[...] = jnp.concatenate(rows, axis=0)

    vt = _nt_dot(wt_ref[D_MODEL:, :], xn).astype(BF16)
    for b in range(nb):
        vt_ref[b] = vt[:, b * MOBA_BLOCK:(b + 1) * MOBA_BLOCK]


def _kv_proj(x, gain, wt_kv, k_gain, cos_t, sin_t):
    s = x.shape[0]
    nb = ROW_TILE // MOBA_BLOCK
    n_blocks = s // MOBA_BLOCK
    half = ROT_DIM // 2
    return pl.pallas_call(
        _kv_proj_kernel,
        grid=(s // ROW_TILE,),
        in_specs=[
            pl.BlockSpec((ROW_TILE, D_MODEL), lambda i: (i, 0)),
            pl.BlockSpec((1, D_MODEL), lambda i: (0, 0)),
            pl.BlockSpec((2 * D_MODEL, D_MODEL), lambda i: (0, 0), pipeline_mode=pl.Buffered(1)),
            pl.BlockSpec((A_DH, A_DH), lambda i: (0, 0)),
            pl.BlockSpec((half, ROW_TILE), lambda i: (0, i)),
            pl.BlockSpec((half, ROW_TILE), lambda i: (0, i)),
        ],
        out_specs=[
            pl.BlockSpec((nb, MOBA_BLOCK, D_MODEL), lambda i: (i, 0, 0)),
            pl.BlockSpec((nb, D_MODEL, MOBA_BLOCK), lambda i: (i, 0, 0)),
            pl.BlockSpec((nb, 1, D_MODEL), lambda i: (i, 0, 0)),
            pl.BlockSpec((nb, 1, D_MODEL), lambda i: (i, 0, 0)),
        ],
        out_shape=[
            jax.ShapeDtypeStruct((n_blocks, MOBA_BLOCK, D_MODEL), BF16),
            jax.ShapeDtypeStruct((n_blocks, D_MODEL, MOBA_BLOCK), BF16),
            jax.ShapeDtypeStruct((n_blocks, 1, D_MODEL), F32),
            jax.ShapeDtypeStruct((n_blocks, 1, D_MODEL), F32),
        ],
        compiler_params=pltpu.CompilerParams(
            dimension_semantics=("parallel",), vmem_limit_bytes=VMEM_LIMIT),
        name="kv_proj",
    )(x, gain, wt_kv, k_gain, cos_t, sin_t)


def _q_proj_kernel(x_ref, g_ref, wt_ref, qg_ref, cos_ref, sin_ref, qt_ref):
    xn = _rms_rows(x_ref[...], g_ref[...]).astype(BF16)
    qt = _headnorm_rope_t(_nt_dot(wt_ref[...], xn), qg_ref[...], cos_ref[...], sin_ref[...], A_DH ** -0.5 * LOG2_E)
    qt_ref[...] = qt.astype(BF16)


def _q_proj(x, gain, wt_q, q_gain, cos_t, sin_t):
    s = x.shape[0]
    half = ROT_DIM // 2
    return pl.pallas_call(
        _q_proj_kernel,
        grid=(s // ROW_TILE,),
        in_specs=[
            pl.BlockSpec((ROW_TILE, D_MODEL), lambda i: (i, 0)),
            pl.BlockSpec((1, D_MODEL), lambda i: (0, 0)),
            pl.BlockSpec((D_MODEL, D_MODEL), lambda i: (0, 0)),
            pl.BlockSpec((A_DH, A_DH), lambda i: (0, 0)),
            pl.BlockSpec((half, ROW_TILE), lambda i: (0, i)),
            pl.BlockSpec((half, ROW_TILE), lambda i: (0, i)),
        ],
        out_specs=pl.BlockSpec((D_MODEL, ROW_TILE), lambda i: (0, i)),
        out_shape=jax.ShapeDtypeStruct((D_MODEL, s), BF16),
        compiler_params=pltpu.CompilerParams(
            dimension_semantics=("parallel",), vmem_limit_bytes=VMEM_LIMIT),
        name="q_proj",
    )(x, gain, wt_q, q_gain, cos_t, sin_t)


def _moba_attn_kernel(qt_ref, k_ref, vt_ref, km_ref, ksq_ref, o_ref, sel_ref, bias_ref, p_ref):
    nb = k_ref.shape[0]
    bs = MOBA_BLOCK
    cur = pl.program_id(1)
    kpos = lax.broadcasted_iota(jnp.int32, (bs, bs), 0)
    qpos = lax.broadcasted_iota(jnp.int32, (bs, bs), 1)
    causal = kpos <= qpos

    qts, own, gaps = [], [], []
    for hh in range(ATTN_HEADS):
        hs = slice(hh * A_DH, (hh + 1) * A_DH)
        qt = qt_ref[hs, :]
        qts.append(qt)
        gate = jnp.dot(km_ref[:, hs], qt, preferred_element_type=F32)
        blk = lax.broadcasted_iota(jnp.int32, gate.shape, 0)
        gate = jnp.where(blk < cur, gate, NEG_INF)
        sel = jnp.zeros(gate.shape, F32)
        for _ in range(MOBA_TOPK):
            top = jnp.max(gate, axis=0, keepdims=True)
            first = jnp.min(jnp.where(gate == top, blk, nb), axis=0, keepdims=True)
            first = jnp.where(top > NEG_INF, first, nb)
            pick = blk == first
            sel = jnp.where(pick, 1.0, sel)
            gate = jnp.where(pick, NEG_INF, gate)
        sel_ref[hh] = sel

        s_own = jnp.dot(k_ref[cur, :, hs], qt, preferred_element_type=F32)
        s_own = jnp.where(causal, s_own, NEG_INF)
        m0 = jnp.max(s_own, axis=0, keepdims=True)
        own.append((m0, jnp.exp2(s_own - m0)))
        bias_ref[hh] = jnp.where(sel > 0.0, -m0, MASKED_SCORE)

        qf = qt.astype(F32)
        q2 = jnp.sum(qf * qf, axis=0, keepdims=True)
        k2 = jnp.max(ksq_ref[:, hs], axis=0, keepdims=True)[:, 0:1]
        gaps.append(jnp.sqrt(q2 * k2) * NORM_BOUND_SLACK - m0)

    def finish(parts):
        o_ref[...] = jnp.concatenate([(acc / l).T for acc, l in parts], axis=1).astype(BF16)

    def fixed_max_path():
        def sublane_partial(p):
            return jnp.sum(p.reshape(p.shape[0] // 8, 8, p.shape[1]), axis=0)

        init = []
        for hh in range(ATTN_HEADS):
            hs = slice(hh * A_DH, (hh + 1) * A_DH)
            p0 = own[hh][1]
            init.append((jnp.dot(vt_ref[cur, hs, :], p0.astype(BF16), preferred_element_type=F32),
                         sublane_partial(p0)))

        def make_body(unroll, base):
            def body(step, carry):
                j0 = base + step * unroll
                out = []
                for hh in range(ATTN_HEADS):
                    hs = slice(hh * A_DH, (hh + 1) * A_DH)
                    acc, lsum = carry[hh]
                    p_parts, vt_parts = [], []
                    for b in range(unroll):
                        s_b = jnp.dot(k_ref[j0 + b, :, hs], qts[hh], preferred_element_type=F32)
                        p_b = jnp.exp2(s_b + bias_ref[hh, pl.ds(j0 + b, 1), :])
                        lsum = lsum + sublane_partial(p_b)
                        p_parts.append(p_b.astype(BF16))
                        vt_parts.append(vt_ref[j0 + b, hs, :])
                    acc = acc + jnp.dot(jnp.concatenate(vt_parts, axis=1), jnp.concatenate(p_parts, axis=0),
                                        preferred_element_type=F32)
                    out.append((acc, lsum))
                return tuple(out)
            return body

        def probs(hh, j0, lsum):
            hs = slice(hh * A_DH, (hh + 1) * A_DH)
            for b in range(FAST_UNROLL):
                s_b = jnp.dot(k_ref[j0 + b, :, hs], qts[hh], preferred_element_type=F32)
                p_b = jnp.exp2(s_b + bias_ref[hh, pl.ds(j0 + b, 1), :])
                lsum = lsum + sublane_partial(p_b)
                p_ref[hh, b * bs:(b + 1) * bs, :] = p_b.astype(BF16)
            return lsum

        def weighted_values(hh, j0, acc):
            hs = slice(hh * A_DH, (hh + 1) * A_DH)
            vt = jnp.concatenate([vt_ref[j0 + b, hs, :] for b in range(FAST_UNROLL)], axis=1)
            return acc + jnp.dot(vt, p_ref[hh], preferred_element_type=F32)

        n_main = jnp.maximum((cur + FAST_UNROLL - TAIL_UNROLL - 1) // FAST_UNROLL, 1)
        carry = tuple((acc, probs(hh, 0, lsum)) for hh, (acc, lsum) in enumerate(init))

        def main_body(step, carry):
            accs = [weighted_values(hh, (step - 1) * FAST_UNROLL, acc) for hh, (acc, _) in enumerate(carry)]
            return tuple((accs[hh], probs(hh, step * FAST_UNROLL, lsum)) for hh, (_, lsum) in enumerate(carry))

        carry = lax.fori_loop(1, n_main, main_body, carry)
        carry = tuple((weighted_values(hh, (n_main - 1) * FAST_UNROLL, acc), lsum)
                      for hh, (acc, lsum) in enumerate(carry))
        done = n_main * FAST_UNROLL
        carry = lax.fori_loop(0, (cur - done + TAIL_UNROLL - 1) // TAIL_UNROLL, make_body(TAIL_UNROLL, done), carry)
        finish([(acc, jnp.sum(lsum, axis=0, keepdims=True)) for acc, lsum in carry])

    def running_max_path():
        init = []
        for hh in range(ATTN_HEADS):
            hs = slice(hh * A_DH, (hh + 1) * A_DH)
            m0, p0 = own[hh]
            init.append((m0, jnp.sum(p0, axis=0, keepdims=True),
                         jnp.dot(vt_ref[cur, hs, :], p0.astype(BF16), preferred_element_type=F32)))

        def body(step, carry):
            j0 = step * ATTN_UNROLL
            out = []
            for hh in range(ATTN_HEADS):
                hs = slice(hh * A_DH, (hh + 1) * A_DH)
                m, l, acc = carry[hh]
                s_parts, vt_parts = [], []
                for b in range(ATTN_UNROLL):
                    s_b = jnp.dot(k_ref[j0 + b, :, hs], qts[hh], preferred_element_type=F32)
                    s_parts.append(jnp.where(sel_ref[hh, pl.ds(j0 + b, 1), :] > 0.0, s_b, NEG_INF))
                    vt_parts.append(vt_ref[j0 + b, hs, :])
                s_t = jnp.concatenate(s_parts, axis=0)
                m_new = jnp.maximum(m, jnp.max(s_t, axis=0, keepdims=True))
                alpha = jnp.exp2(m - m_new)
                p = jnp.exp2(s_t - m_new)
                l = alpha * l + jnp.sum(p, axis=0, keepdims=True)
                acc = alpha * acc + jnp.dot(jnp.concatenate(vt_parts, axis=1), p.astype(BF16),
                                            preferred_element_type=F32)
                out.append((m_new, l, acc))
            return tuple(out)

        final = lax.fori_loop(0, (cur + ATTN_UNROLL - 1) // ATTN_UNROLL, body, tuple(init))
        finish([(acc, l) for _, l, acc in final])

    worst_gap = jnp.max(functools.reduce(jnp.maximum, gaps))
    lax.cond(worst_gap <= MAX_EXP2_ARG, fixed_max_path, running_max_path)


def _moba_attn(qt, kb, vtb, kmean, ksq):
    s = qt.shape[1]
    nb = s // MOBA_BLOCK
    gw = ATTN_HEADS * A_DH
    return pl.pallas_call(
        _moba_attn_kernel,
        grid=(A_HEADS // ATTN_HEADS, nb),
        in_specs=[
            pl.BlockSpec((gw, MOBA_BLOCK), lambda g, i: (g, i)),
            pl.BlockSpec((nb, MOBA_BLOCK, gw), lambda g, i: (0, 0, g), pipeline_mode=pl.Buffered(1)),
            pl.BlockSpec((nb, gw, MOBA_BLOCK), lambda g, i: (0, g, 0), pipeline_mode=pl.Buffered(1)),
            pl.BlockSpec((nb, gw), lambda g, i: (0, g)),
            pl.BlockSpec((nb, gw), lambda g, i: (0, g)),
        ],
        out_specs=pl.BlockSpec((MOBA_BLOCK, gw), lambda g, i: (i, g)),
        out_shape=jax.ShapeDtypeStruct((s, D_MODEL), BF16),
        scratch_shapes=[pltpu.VMEM((ATTN_HEADS, nb, MOBA_BLOCK), F32),
                        pltpu.VMEM((ATTN_HEADS, nb, MOBA_BLOCK), F32),
                        pltpu.VMEM((ATTN_HEADS, FAST_UNROLL * MOBA_BLOCK, MOBA_BLOCK), BF16)],
        compiler_params=pltpu.CompilerParams(
            dimension_semantics=("parallel", "arbitrary"), vmem_limit_bytes=VMEM_LIMIT),
        name="moba_attn",
    )(qt, kb, vtb, kmean, ksq)


def _rope_tables_t(s_len):
    pos = jnp.arange(s_len, dtype=F32)
    inv = ROPE_THETA ** (-jnp.arange(0, ROT_DIM, 2, dtype=F32) / ROT_DIM)
    ang = inv[:, None] * pos[None, :]
    return jnp.cos(ang), jnp.sin(ang)


def _lane_replicated(gain):
    return jnp.broadcast_to(gain[:, None], (gain.shape[0], 128))


def kernel(x, a_norm, a_w_in, a_b_gates, a_h_norm, a_w_out, kv_norm, w_kv, k_norm, b_norm, b_w_q, b_q_norm,
           b_w_o, f_norm, f_w_up, f_conv_w, f_conv_b, f_w_down):
    b_, s_, d_ = x.shape
    assert (b_, d_) == (1, D_MODEL) and s_ % ROW_TILE == 0 and s_ % MOBA_BLOCK == 0
    cos_t, sin_t = _rope_tables_t(s_)
    h = x.reshape(s_, d_)
    kb = vtb = kmean = ksq = None
    for l in range(DEPTH):
        mixer = ()
        if l < N_A:
            w_in = a_w_in[l]
            wg = w_in[:, M_QKVO:]
            wg_hi = wg.astype(BF16)
            wg_lo = (wg - wg_hi.astype(F32)).astype(BF16)
            dk = M_HEADS * M_DQK
            wt_voqg = jnp.concatenate([w_in[:, 2 * dk:M_QKVO].astype(BF16), w_in[:, :dk].astype(BF16),
                                       wg_hi, wg_lo], axis=1).T
            k, ut, gates_t = _mlstm_in_proj(h, a_norm[l][None], w_in[:, dk:2 * dk].astype(BF16), wt_voqg,
                                            a_b_gates[l].reshape(2 * M_HEADS, 1))
            h = _mlstm_core(h, k, ut, gates_t, _lane_replicated(a_h_norm[l]), a_w_out[l].T.astype(BF16))
        else:
            j = l - N_A
            qt = _q_proj(h, b_norm[j][None], b_w_q[j].T.astype(BF16), _lane_replicated(b_q_norm[j]), cos_t, sin_t)
            mixer = (_moba_attn(qt, kb, vtb, kmean, ksq), b_w_o[j].astype(BF16))
        h = _conv_ffn(h, f_norm[l][None], f_w_up[l].astype(BF16), f_conv_w[l], f_conv_b[l][None],
                      f_w_down[l].astype(BF16), *mixer)
        if l == N_A - 1:
            kb, vtb, km, ks = _kv_proj(h, kv_norm[None], w_kv.T.astype(BF16), _lane_replicated(k_norm), cos_t, sin_t)
            kmean = km.reshape(s_ // MOBA_BLOCK, D_MODEL).astype(BF16)
            ksq = ks.reshape(s_ // MOBA_BLOCK, D_MODEL)
    return h.reshape(b_, s_, d_)
```
